```python
import jax
import jax.numpy as jnp
from jax import lax
import numpy as np

D_MODEL = 4096
BATCH = 1
SEQ = 16384
DEPTH = 4

CTX_LEN = 256
GRID_W = 64
N_MIXERS = 3
ROPE_BASE = 10000.0
NORM_EPS = 1e-6
ADA_RANK = 256
Q_BLOCK = 128
TOKEN_BLOCK = 128

MLA_HEADS = 64
MLA_Q_RANK = 896
MLA_KV_RANK = 512
MLA_NOPE_DIM = 128
MLA_ROPE_DIM = 64
MLA_V_DIM = 128
MLA_QK_DIM = MLA_NOPE_DIM + MLA_ROPE_DIM

RWKV_HEAD = 64
RWKV_HEADS = D_MODEL // RWKV_HEAD
RWKV_DECAY_LORA = 128
RWKV_AAA_LORA = 128
RWKV_GATE_LORA = 480
RWKV_GN_EPS = 64e-5

SWA_HEADS = 64
SWA_KV_HEADS = 8
SWA_HEAD_DIM = 64
SWA_WINDOW = 128
SWA_Q_WIDTH = SWA_HEADS * SWA_HEAD_DIM
SWA_KV_WIDTH = SWA_KV_HEADS * SWA_HEAD_DIM

N_EXPERTS = 32
TOP_K = 4
EXPERT_FF = 384
SWIGLU_ALPHA = 1.702
SWIGLU_LIMIT = 7.0

kernel_name = 'hybrid_mla_rwkv7_swa_moe_diffusion_trunk'


def rms_norm(x, g):
    xf = x.astype(jnp.float32)
    y = xf * lax.rsqrt(jnp.mean(xf * xf, axis=-1, keepdims=True) + NORM_EPS)
    return (y * g.astype(jnp.float32)).astype(x.dtype)


def modulate(h, shift, scale):
    return h * (1.0 + scale) + shift


def adaln_modulation(cond, a, b, bias):
    m = (jax.nn.silu(cond) @ a) @ b + bias
    return jnp.split(m[:, None, :], 6, axis=-1)


def axial_rope_angles(n_tok, rot_dim):
    t = jnp.arange(n_tok)
    row = (t // GRID_W).astype(jnp.float32)
    col = (t % GRID_W).astype(jnp.float32)
    axis_dim = rot_dim // 2
    inv = ROPE_BASE ** (-jnp.arange(0, axis_dim, 2, dtype=jnp.float32) / axis_dim)
    ang = jnp.concatenate([row[:, None] * inv, col[:, None] * inv], axis=-1)
    return jnp.cos(ang), jnp.sin(ang)


def apply_rope(x, cos, sin):
    half = x.shape[-1] // 2
    x1 = x[..., :half].astype(jnp.float32)
    x2 = x[..., half:].astype(jnp.float32)
    c = cos[:, None, :]
    s = sin[:, None, :]
    return jnp.concatenate([x1 * c - x2 * s, x1 * s + x2 * c], axis=-1).astype(x.dtype)


def dense_block_attention(q, k, v, scale):
    b, sq, h, dk = q.shape
    nb = sq // Q_BLOCK
    qb = jnp.moveaxis(q.reshape(b, nb, Q_BLOCK, h, dk), 1, 0)

    def block(qblk):
        s = jnp.einsum('bqhd,bkhd->bhqk', qblk, k, preferred_element_type=jnp.float32) * scale
        p = jax.nn.softmax(s, axis=-1).astype(v.dtype)
        return jnp.einsum('bhqk,bkhd->bqhd', p, v)

    o = lax.map(block, qb)
    return jnp.moveaxis(o, 0, 1).reshape(b, sq, h, v.shape[-1])


def mla_mixer(h_ctx, h_lat, w_dq, q_norm, w_uq, w_dkv, kv_norm, w_ukv, w_o, with_ctx):
    b, s, _ = h_lat.shape
    n_ctx = h_ctx.shape[1]
    cos, sin = axial_rope_angles(s, MLA_ROPE_DIM)

    def queries(h, rotate):
        n = h.shape[1]
        q = (rms_norm(h @ w_dq, q_norm) @ w_uq).reshape(b, n, MLA_HEADS, MLA_QK_DIM)
        if not rotate:
            return q
        return jnp.concatenate([q[..., :MLA_NOPE_DIM], apply_rope(q[..., MLA_NOPE_DIM:], cos, sin)], axis=-1)

    def keys_values(h, rotate):
        n = h.shape[1]
        ckv = h @ w_dkv
        kv = (rms_norm(ckv[..., :MLA_KV_RANK], kv_norm) @ w_ukv).reshape(b, n, MLA_HEADS, MLA_NOPE_DIM + MLA_V_DIM)
        k_pe = ckv[..., MLA_KV_RANK:][:, :, None, :]
        if rotate:
            k_pe = apply_rope(k_pe, cos, sin)
        k = jnp.concatenate([kv[..., :MLA_NOPE_DIM], jnp.broadcast_to(k_pe, (b, n, MLA_HEADS, MLA_ROPE_DIM))], axis=-1)
        return k, kv[..., MLA_NOPE_DIM:]

    scale = MLA_QK_DIM ** -0.5
    k_c, v_c = keys_values(h_ctx, False)
    k_l, v_l = keys_values(h_lat, True)
    o_l = dense_block_attention(queries(h_lat, True), jnp.concatenate([k_c, k_l], axis=1),
                                jnp.concatenate([v_c, v_l], axis=1), scale)
    y_l = o_l.reshape(b, s, MLA_HEADS * MLA_V_DIM) @ w_o
    y_c = None
    if with_ctx:
        o_c = dense_block_attention(queries(h_ctx, False), k_c, v_c, scale)
        y_c = o_c.reshape(b, n_ctx, MLA_HEADS * MLA_V_DIM) @ w_o
    return y_c, y_l


def centred_shift(x):
    zero = jnp.zeros_like(x[:, :1])
    prev = jnp.concatenate([zero, x[:, :-1]], axis=1)
    nxt = jnp.concatenate([x[:, 1:], zero], axis=1)
    return 0.5 * (prev + nxt) - x


def wkv_scan(state0, r, w, k, v, a, bv, reverse):
    def step(S, inp):
        r_t, w_t, k_t, v_t, a_t, b_t = inp
        sa = jnp.einsum('bhvk,bhk->bhv', S, a_t)
        S = S * w_t[:, :, None, :] + sa[..., None] * b_t[:, :, None, :] + v_t[..., None] * k_t[:, :, None, :]
        return S, jnp.einsum('bhvk,bhk->bhv', S, r_t)

    xs = tuple(jnp.moveaxis(z, 1, 0) for z in (r, w, k, v, a, bv))
    S, y = lax.scan(step, state0, xs, reverse=reverse)
    return S, jnp.moveaxis(y, 0, 1)


def rwkv7_mixer(h_ctx, h_lat, mu, w_rkv, w0, w1, w2, a0, a1, a2, g1, g2, k_k, k_a, r_k, ln_w, ln_b, w_o, with_ctx):
    hd = (RWKV_HEADS, RWKV_HEAD)

    def heads(z):
        return z.reshape(z.shape[:-1] + hd).astype(jnp.float32)

    def prepare(h):
        xx = centred_shift(h)
        mix = lambda i: h + xx * mu[i]
        r = heads(mix(0) @ w_rkv[0])
        k = heads(mix(1) @ w_rkv[1])
        v = heads(mix(2) @ w_rkv[2])
        kk = k * k_k.reshape(hd)
        kk = kk * lax.rsqrt(jnp.maximum(jnp.sum(kk * kk, axis=-1, keepdims=True), 1e-24))
        g = jax.nn.sigmoid(mix(5) @ g1) @ g2
        return r, k, v, kk, mix(3), mix(4), g

    def direction(z, k, kk, xw, xa):
        logw = -jax.nn.softplus(-(w0[z] + jnp.tanh(xw @ w1[z]) @ w2[z])) - 0.5
        decay = jnp.exp(-jnp.exp(heads(logw)))
        a = jax.nn.sigmoid(heads(a0[z] + (xa @ a1[z]) @ a2[z]))
        kz = k * (1.0 + (a - 1.0) * k_a.reshape(hd))
        return decay, kz, -kk, kk * a

    def finish(y, bonus, g, dtype):
        b, n = y.shape[:2]
        mean = jnp.mean(y, axis=-1, keepdims=True)
        var = jnp.mean(jnp.square(y - mean), axis=-1, keepdims=True)
        yn = ((y - mean) * lax.rsqrt(var + RWKV_GN_EPS)).reshape(b, n, D_MODEL) * ln_w + ln_b
        out = (yn + bonus.reshape(b, n, D_MODEL)).astype(dtype)
        return (out * g) @ w_o

    rc, kc, vc, kkc, xwc, xac, gc = prepare(h_ctx)
    rl, kl, vl, kkl, xwl, xal, gl = prepare(h_lat)
    b = h_lat.shape[0]
    state0 = jnp.zeros((b, RWKV_HEADS, RWKV_HEAD, RWKV_HEAD), jnp.float32)
    rk = r_k.reshape(hd).astype(jnp.float32)
    wkv_c, wkv_l, bonus_c, bonus_l = [], [], [], []
    for z in range(2):
        rev = z == 1
        dc, kzc, ac, bc = direction(z, kc, kkc, xwc, xac)
        s_ctx, yc = wkv_scan(state0, rc, dc, kzc, vc, ac, bc, rev)
        dl, kzl, al, bl = direction(z, kl, kkl, xwl, xal)
        _, yl = wkv_scan(s_ctx, rl, dl, kzl, vl, al, bl, rev)
        wkv_c.append(yc)
        wkv_l.append(yl)
        bonus_c.append(jnp.sum(rc * kzc * rk, axis=-1, keepdims=True) * vc)
        bonus_l.append(jnp.sum(rl * kzl * rk, axis=-1, keepdims=True) * vl)
    y_l = finish(wkv_l[0] + wkv_l[1], bonus_l[0] + bonus_l[1], gl, h_lat.dtype)
    y_c = None
    if with_ctx:
        y_c = finish(wkv_c[0] + wkv_c[1], bonus_c[0] + bonus_c[1], gc, h_ctx.dtype)
    return y_c, y_l


def swa_mixer(h_ctx, h_lat, w_qkv, b_qkv, sink, w_o, b_o, with_ctx):
    b, s, _ = h_lat.shape
    n_ctx = h_ctx.shape[1]
    grp = SWA_HEADS // SWA_KV_HEADS
    cos, sin = axial_rope_angles(s, SWA_HEAD_DIM)

    def project(h, rotate):
        n = h.shape[1]
        qkv = h @ w_qkv + b_qkv
        q = qkv[..., :SWA_Q_WIDTH].reshape(b, n, SWA_HEADS, SWA_HEAD_DIM)
        k = qkv[..., SWA_Q_WIDTH:SWA_Q_WIDTH + SWA_KV_WIDTH].reshape(b, n, SWA_KV_HEADS, SWA_HEAD_DIM)
        v = qkv[..., SWA_Q_WIDTH + SWA_KV_WIDTH:].reshape(b, n, SWA_KV_HEADS, SWA_HEAD_DIM)
        if rotate:
            q = apply_rope(q, cos, sin)
            k = apply_rope(k, cos, sin)
        return q.reshape(b, n, SWA_KV_HEADS, grp, SWA_HEAD_DIM), k, v

    scale = SWA_HEAD_DIM ** -0.5
    sink_l = sink.reshape(SWA_KV_HEADS, grp).astype(jnp.float32)

    def softmax_with_sink(logits):
        sk = jnp.broadcast_to(sink_l[None, :, :, None, None], logits.shape[:-1] + (1,))
        p = jax.nn.softmax(jnp.concatenate([logits, sk], axis=-1), axis=-1)
        return p[..., :-1]

    _, k_c, v_c = project(h_ctx, False)
    q_l, k_l, v_l = project(h_lat, True)
    nb = s // Q_BLOCK
    pad = ((0, 0), (Q_BLOCK, Q_BLOCK), (0, 0), (0, 0))
    kp = jnp.pad(k_l, pad)
    vp = jnp.pad(v_l, pad)
    rel = jnp.arange(3 * Q_BLOCK)[None, :] - Q_BLOCK - jnp.arange(Q_BLOCK)[:, None]
    band = jnp.abs(rel) <= SWA_WINDOW

    def block(args):
        j, qb = args
        start = j * Q_BLOCK
        kw = lax.dynamic_slice_in_dim(kp, start, 3 * Q_BLOCK, axis=1)
        vw = lax.dynamic_slice_in_dim(vp, start, 3 * Q_BLOCK, axis=1)
        kpos = start - Q_BLOCK + jnp.arange(3 * Q_BLOCK)
        mask = band & ((kpos >= 0) & (kpos < s))[None, :]
        s_ctx = jnp.einsum('bqkgd,bnkd->bkgqn', qb, k_c, preferred_element_type=jnp.float32) * scale
        s_win = jnp.einsum('bqkgd,bnkd->bkgqn', qb, kw, preferred_element_type=jnp.float32) * scale
        s_win = jnp.where(mask, s_win, -jnp.inf)
        p = softmax_with_sink(jnp.concatenate([s_ctx, s_win], axis=-1))
        return (jnp.einsum('bkgqn,bnkd->bqkgd', p[..., :n_ctx].astype(v_c.dtype), v_c)
                + jnp.einsum('bkgqn,bnkd->bqkgd', p[..., n_ctx:].astype(vw.dtype), vw))

    qbs = jnp.moveaxis(q_l.reshape(b, nb, Q_BLOCK, SWA_KV_HEADS, grp, SWA_HEAD_DIM), 1, 0)
    o_l = jnp.moveaxis(lax.map(block, (jnp.arange(nb), qbs)), 0, 1).reshape(b, s, SWA_Q_WIDTH)
    y_l = o_l @ w_o + b_o
    y_c = None
    if with_ctx:
        q_c = project(h_ctx, False)[0]
        p = softmax_with_sink(jnp.einsum('bqkgd,bnkd->bkgqn', q_c, k_c, preferred_element_type=jnp.float32) * scale)
        o_c = jnp.einsum('bkgqn,bnkd->bqkgd', p.astype(v_c.dtype), v_c).reshape(b, n_ctx, SWA_Q_WIDTH)
        y_c = o_c @ w_o + b_o
    return y_c, y_l


def moe_ffn(h, w_router, b_router, w1, b1, w2, b2):
    b, n, d = h.shape
    blocks = h.reshape(b * n // TOKEN_BLOCK, TOKEN_BLOCK, d)

    def block(hb):
        logits = jnp.einsum('td,de->te', hb, w_router, preferred_element_type=jnp.float32) + b_router.astype(jnp.float32)
        top_val, top_idx = lax.top_k(logits, TOP_K)
        weights = jax.nn.softmax(top_val, axis=-1)
        gate = jnp.einsum('tk,tke->te', weights, jax.nn.one_hot(top_idx, N_EXPERTS, dtype=jnp.float32)).astype(hb.dtype)
        u = jnp.einsum('td,edf->tef', hb, w1) + b1
        x_glu = jnp.minimum(u[..., :EXPERT_FF], SWIGLU_LIMIT)
        x_lin = jnp.clip(u[..., EXPERT_FF:], -SWIGLU_LIMIT, SWIGLU_LIMIT)
        act = x_glu * jax.nn.sigmoid(SWIGLU_ALPHA * x_glu) * (x_lin + 1.0)
        return jnp.einsum('tef,efd->td', gate[..., None] * act, w2) + gate @ b2

    return lax.map(block, blocks).reshape(b, n, d)


def setup_inputs(seed: int = 0) -> dict:
    key = jax.random.key(seed)
    ks = iter(jax.random.split(key, 64))

    def nrm(shape, scale):
        return jax.random.normal(next(ks), shape, jnp.float32) * scale

    def gain(shape):
        return 1.0 + nrm(shape, 0.05)

    D = D_MODEL
    n_mla = len(range(0, DEPTH, N_MIXERS))
    n_rwkv = len(range(1, DEPTH, N_MIXERS))
    n_swa = len(range(2, DEPTH, N_MIXERS))
    inputs = {
        'x': nrm((BATCH, SEQ, D), 1.0),
        'c': nrm((BATCH, D), 1.0),
        'ctx': nrm((BATCH, CTX_LEN, D), 1.0),
        'c_ctx': nrm((D,), 1.0),
        'ada_a': nrm((DEPTH, D, ADA_RANK), D ** -0.5),
        'ada_b': nrm((DEPTH, ADA_RANK, 6 * D), 0.2 * ADA_RANK ** -0.5),
        'ada_bias': nrm((DEPTH, 6 * D), 0.02),
        'norm_g': gain((DEPTH, 4, D)),
        'mla_w_dq': nrm((n_mla, D, MLA_Q_RANK), D ** -0.5),
        'mla_q_norm': gain((n_mla, MLA_Q_RANK)),
        'mla_w_uq': nrm((n_mla, MLA_Q_RANK, MLA_HEADS * MLA_QK_DIM), MLA_Q_RANK ** -0.5),
        'mla_w_dkv': nrm((n_mla, D, MLA_KV_RANK + MLA_ROPE_DIM), D ** -0.5),
        'mla_kv_norm': gain((n_mla, MLA_KV_RANK)),
        'mla_w_ukv': nrm((n_mla, MLA_KV_RANK, MLA_HEADS * (MLA_NOPE_DIM + MLA_V_DIM)), MLA_KV_RANK ** -0.5),
        'mla_w_o': nrm((n_mla, MLA_HEADS * MLA_V_DIM, D), (MLA_HEADS * MLA_V_DIM) ** -0.5),
        'rwkv_mu': jax.random.uniform(next(ks), (n_rwkv, 6, D), jnp.float32),
        'rwkv_w_rkv': nrm((n_rwkv, 3, D, D), D ** -0.5),
        'rwkv_w0': jax.random.uniform(next(ks), (n_rwkv, 2, D), jnp.float32, minval=-6.0, maxval=1.0),
        'rwkv_w1': nrm((n_rwkv, 2, D, RWKV_DECAY_LORA), 0.1 * D ** -0.5),
        'rwkv_w2': nrm((n_rwkv, 2, RWKV_DECAY_LORA, D), 0.1 * RWKV_DECAY_LORA ** -0.5),
        'rwkv_a0': nrm((n_rwkv, 2, D), 0.1),
        'rwkv_a1': nrm((n_rwkv, 2, D, RWKV_AAA_LORA), 0.1 * D ** -0.5),
        'rwkv_a2': nrm((n_rwkv, 2, RWKV_AAA_LORA, D), 0.1 * RWKV_AAA_LORA ** -0.5),
        'rwkv_g1': nrm((n_rwkv, D, RWKV_GATE_LORA), D ** -0.5),
        'rwkv_g2': nrm((n_rwkv, RWKV_GATE_LORA, D), RWKV_GATE_LORA ** -0.5),
        'rwkv_k_k': 0.85 + nrm((n_rwkv, D), 0.05),
        'rwkv_k_a': gain((n_rwkv, D)),
        'rwkv_r_k': nrm((n_rwkv, D), 0.1),
        'rwkv_ln_w': gain((n_rwkv, D)),
        'rwkv_ln_b': nrm((n_rwkv, D), 0.02),
        'rwkv_w_o': nrm((n_rwkv, D, D), D ** -0.5),
        'swa_w_qkv': nrm((n_swa, D, SWA_Q_WIDTH + 2 * SWA_KV_WIDTH), D ** -0.5),
        'swa_b_qkv': nrm((n_swa, SWA_Q_WIDTH + 2 * SWA_KV_WIDTH), 0.02),
        'swa_sink': nrm((n_swa, SWA_HEADS), 0.5),
        'swa_w_o': nrm((n_swa, SWA_Q_WIDTH, D), SWA_Q_WIDTH ** -0.5),
        'swa_b_o': nrm((n_swa, D), 0.02),
        'moe_w_router': nrm((DEPTH, D, N_EXPERTS), D ** -0.5),
        'moe_b_router': nrm((DEPTH, N_EXPERTS), 0.01),
        'moe_w1': nrm((DEPTH, N_EXPERTS, D, 2 * EXPERT_FF), D ** -0.5),
        'moe_b1': nrm((DEPTH, N_EXPERTS, 2 * EXPERT_FF), 0.02),
        'moe_w2': nrm((DEPTH, N_EXPERTS, EXPERT_FF, D), EXPERT_FF ** -0.5),
        'moe_b2': nrm((DEPTH, N_EXPERTS, D), 0.02),
    }
    return inputs


def reference(x, c, ctx, c_ctx, ada_a, ada_b, ada_bias, norm_g,
              mla_w_dq, mla_q_norm, mla_w_uq, mla_w_dkv, mla_kv_norm, mla_w_ukv, mla_w_o,
              rwkv_mu, rwkv_w_rkv, rwkv_w0, rwkv_w1, rwkv_w2, rwkv_a0, rwkv_a1, rwkv_a2,
              rwkv_g1, rwkv_g2, rwkv_k_k, rwkv_k_a, rwkv_r_k, rwkv_ln_w, rwkv_ln_b, rwkv_w_o,
              swa_w_qkv, swa_b_qkv, swa_sink, swa_w_o, swa_b_o,
              moe_w_router, moe_b_router, moe_w1, moe_b1, moe_w2, moe_b2):
    xl = x
    xc = ctx
    cond_ctx = c_ctx[None, :]
    for i in range(DEPTH):
        last = i == DEPTH - 1
        kind = i % N_MIXERS
        m = i // N_MIXERS
        mod_l = adaln_modulation(c, ada_a[i], ada_b[i], ada_bias[i])
        mod_c = adaln_modulation(cond_ctx, ada_a[i], ada_b[i], ada_bias[i])
        h_l = modulate(rms_norm(xl, norm_g[i, 0]), mod_l[0], mod_l[1])
        h_c = modulate(rms_norm(xc, norm_g[i, 0]), mod_c[0], mod_c[1])
        if kind == 0:
            y_c, y_l = mla_mixer(h_c, h_l, mla_w_dq[m], mla_q_norm[m], mla_w_uq[m], mla_w_dkv[m],
                                 mla_kv_norm[m], mla_w_ukv[m], mla_w_o[m], not last)
        elif kind == 1:
            y_c, y_l = rwkv7_mixer(h_c, h_l, rwkv_mu[m], rwkv_w_rkv[m], rwkv_w0[m], rwkv_w1[m], rwkv_w2[m],
                                   rwkv_a0[m], rwkv_a1[m], rwkv_a2[m], rwkv_g1[m], rwkv_g2[m],
                                   rwkv_k_k[m], rwkv_k_a[m], rwkv_r_k[m], rwkv_ln_w[m], rwkv_ln_b[m],
                                   rwkv_w_o[m], not last)
        else:
            y_c, y_l = swa_mixer(h_c, h_l, swa_w_qkv[m], swa_b_qkv[m], swa_sink[m], swa_w_o[m], swa_b_o[m], not last)
        xl = xl + mod_l[2] * rms_norm(y_l, norm_g[i, 1])
        f_l = moe_ffn(modulate(rms_norm(xl, norm_g[i, 2]), mod_l[3], mod_l[4]),
                      moe_w_router[i], moe_b_router[i], moe_w1[i], moe_b1[i], moe_w2[i], moe_b2[i])
        xl = xl + mod_l[5] * rms_norm(f_l, norm_g[i, 3])
        if not last:
            xc = xc + mod_c[2] * rms_norm(y_c, norm_g[i, 1])
            f_c = moe_ffn(modulate(rms_norm(xc, norm_g[i, 2]), mod_c[3], mod_c[4]),
                          moe_w_router[i], moe_b_router[i], moe_w1[i], moe_b1[i], moe_w2[i], moe_b2[i])
            xc = xc + mod_c[5] * rms_norm(f_c, norm_g[i, 3])
    return xl
```

```python
import functools

import jax
import jax.numpy as jnp
import numpy as np
from jax import lax
from jax.experimental import pallas as pl
from jax.experimental.pallas import tpu as pltpu

F32 = jnp.float32
BF16 = jnp.bfloat16

GRID_W = 64
ROPE_BASE = 10000.0
NORM_EPS = 1e-6
MLA_HEADS = 64
MLA_KV_RANK = 512
MLA_NOPE_DIM = 128
MLA_ROPE_DIM = 64
MLA_V_DIM = 128
MLA_QK_DIM = MLA_NOPE_DIM + MLA_ROPE_DIM
RWKV_HEAD = 64
RWKV_GN_EPS = 64e-5
SWA_HEADS = 64
SWA_KV_HEADS = 8
SWA_HEAD_DIM = 64
SWA_WINDOW = 128
SWA_BLOCK = 128
N_EXPERTS = 32
TOP_K = 4
EXPERT_FF = 384
SWIGLU_ALPHA = 1.702
SWIGLU_LIMIT = 7.0
N_MIXERS = 3

LANES = 128
SUBLANES = 8
VMEM_LIMIT = 56 * 1024 * 1024

MM_ROWS = 1280
EW_ROWS = 256


def _params(sem):
    return pltpu.CompilerParams(dimension_semantics=sem, vmem_limit_bytes=VMEM_LIMIT)


def _row_tile(t, pref):
    if t <= pref:
        return t
    best = None
    for cand in range(pref, 7, -8):
        if t % cand == 0:
            best = cand
            break
    assert best is not None, (t, pref)
    return best


def _col_tile(n, pref):
    if n <= pref:
        return n
    for cand in range(pref, LANES - 1, -LANES):
        if n % cand == 0:
            return cand
    return n


def _mm_body(*refs, nk, n_epi, epi, store):
    x_ref, w_ref = refs[0], refs[1]
    epi_refs = refs[2:2 + n_epi]
    o_ref = refs[2 + n_epi]
    ids = (pl.program_id(0), pl.program_id(1))
    part = jnp.dot(x_ref[...].astype(BF16), w_ref[...], preferred_element_type=F32)
    if nk == 1:
        store(o_ref, epi(part, ids, *epi_refs))
        return
    acc_ref = refs[3 + n_epi]
    k = pl.program_id(2)

    @pl.when(k == 0)
    def _():
        acc_ref[...] = part

    @pl.when(k > 0)
    def _():
        acc_ref[...] += part

    @pl.when(k == nk - 1)
    def _():
        store(o_ref, epi(acc_ref[...], ids, *epi_refs))


def _store_plain(o_ref, val):
    o_ref[...] = val.astype(o_ref.dtype)


def _epi_none(acc, ids):
    return acc


def _matmul(x, w, *, out_dtype, tm=None, tn=None, tk=None, epi=_epi_none, epi_ops=(),
            w_spec=None, n_total=None, out_shape=None, out_spec=None, store=_store_plain, name="mm"):
    m, kdim = x.shape
    n = n_total if n_total is not None else w.shape[1]
    tm = tm or _row_tile(m, MM_ROWS)
    tn = tn or _col_tile(n, 1024)
    tk = tk or (kdim if kdim <= 1024 else _col_tile(kdim, 1024))
    assert m % tm == 0 and n % tn == 0 and kdim % tk == 0, (m, n, kdim, tm, tn, tk)
    nk = kdim // tk
    grid = (m // tm, n // tn, nk)
    if w_spec is None:
        w_spec = pl.BlockSpec((tk, tn), lambda i, j, k: (k, j))
    in_specs = [pl.BlockSpec((tm, tk), lambda i, j, k: (i, k)), w_spec] + [s for _, s in epi_ops]
    if out_shape is None:
        out_shape = jax.ShapeDtypeStruct((m, n), out_dtype)
        out_spec = pl.BlockSpec((tm, tn), lambda i, j, k: (i, j))
    scratch = [pltpu.VMEM((tm, tn), F32)] if nk > 1 else []
    body = functools.partial(_mm_body, nk=nk, n_epi=len(epi_ops), epi=epi, store=store)
    return pl.pallas_call(
        body, grid=grid, in_specs=in_specs, out_specs=out_spec, out_shape=out_shape,
        scratch_shapes=scratch, name=name,
        compiler_params=_params(("parallel", "parallel", "arbitrary")),
    )(x, w, *[a for a, _ in epi_ops])


def _row_vec_op(vec, tn):
    return (vec.reshape(1, -1).astype(F32), pl.BlockSpec((1, tn), lambda i, j, k: (0, j)))


def _epi_bias(acc, ids, b_ref):
    return acc + b_ref[...]


def _epi_bias_act(acc, ids, b_ref, *, act):
    return act(acc + b_ref[...])


def _epi_act(acc, ids, *, act):
    return act(acc)


def _epi_rmsnorm(acc, ids, g_ref):
    ms = jnp.mean(acc * acc, axis=-1, keepdims=True)
    return acc * lax.rsqrt(ms + NORM_EPS) * g_ref[...]


def _rope_cols(x, c, sg):
    lane = lax.broadcasted_iota(jnp.int32, (1, LANES), 1)
    first_half = (lane % 64) < 32
    outs = []
    for g in range(x.shape[1] // LANES):
        xg = x[:, g * LANES:(g + 1) * LANES]
        partner = jnp.where(first_half, pltpu.roll(xg, LANES - 32, 1), pltpu.roll(xg, 32, 1))
        outs.append(xg * c + partner * sg)
    return outs[0] if len(outs) == 1 else jnp.concatenate(outs, axis=1)


def _epi_rope(acc, ids, c_ref, s_ref):
    return _rope_cols(acc, c_ref[...], s_ref[...])


def _epi_bias_rope(acc, ids, b_ref, c_ref, s_ref):
    return _rope_cols(acc + b_ref[...], c_ref[...], s_ref[...])


def _rope_ops(rope_c, rope_s, tm):
    spec = pl.BlockSpec((tm, LANES), lambda i, j, k: (i, 0))
    return [(rope_c, spec), (rope_s, spec)]


def _rope_tables(s_lat, n_ctx):
    t = jnp.arange(s_lat)
    row = (t // GRID_W).astype(F32)
    col = (t % GRID_W).astype(F32)
    axis_dim = 32
    inv = ROPE_BASE ** (-jnp.arange(0, axis_dim, 2, dtype=F32) / axis_dim)
    ang = jnp.concatenate([row[:, None] * inv, col[:, None] * inv], axis=-1)
    cos, sin = jnp.cos(ang), jnp.sin(ang)
    c64 = jnp.concatenate([cos, cos], axis=-1)
    s64 = jnp.concatenate([-sin, sin], axis=-1)
    c = jnp.concatenate([c64, c64], axis=-1)
    s = jnp.concatenate([s64, s64], axis=-1)
    c = jnp.concatenate([c, jnp.ones((n_ctx, LANES), F32)], axis=0)
    s = jnp.concatenate([s, jnp.zeros((n_ctx, LANES), F32)], axis=0)
    return c, s


def _rms(x, g):
    return x * lax.rsqrt(jnp.mean(x * x, axis=-1, keepdims=True) + NORM_EPS) * g


def _mod_spec(nb_lat):
    return pl.BlockSpec((1, 6, None), lambda i: (jnp.where(i >= nb_lat, 1, 0), 0, 0))


def _norm_mod_body(x_ref, g_ref, mod_ref, h_ref):
    h = _rms(x_ref[...], g_ref[...]) * (1.0 + mod_ref[0, 1:2, :]) + mod_ref[0, 0:1, :]
    h_ref[...] = h.astype(h_ref.dtype)


def _norm_mod(x, g, mods, s_lat):
    t, d = x.shape
    tm = EW_ROWS
    nb_lat = s_lat // tm
    return pl.pallas_call(
        _norm_mod_body, grid=(t // tm,),
        in_specs=[pl.BlockSpec((tm, d), lambda i: (i, 0)),
                  pl.BlockSpec((1, d), lambda i: (0, 0)),
                  pl.BlockSpec((1, 6, d), lambda i: (jnp.where(i >= nb_lat, 1, 0), 0, 0))],
        out_specs=pl.BlockSpec((tm, d), lambda i: (i, 0)),
        out_shape=jax.ShapeDtypeStruct((t, d), BF16), name="norm_mod",
        compiler_params=_params(("parallel",)),
    )(x, g.reshape(1, d), mods)


def _top4_gates(logits):
    lane = lax.broadcasted_iota(jnp.int32, logits.shape, 1)
    work = logits
    gate = jnp.zeros_like(logits)
    denom = jnp.zeros((logits.shape[0], 1), F32)
    m0 = None
    for _ in range(TOP_K):
        m = jnp.max(work, axis=1, keepdims=True)
        idx = jnp.min(jnp.where(work == m, lane, N_EXPERTS), axis=1, keepdims=True)
        hit = lane == idx
        if m0 is None:
            m0 = m
        wk = jnp.exp(m - m0)
        gate = gate + jnp.where(hit, wk, 0.0)
        denom = denom + wk
        work = jnp.where(hit, -jnp.inf, work)
    return gate / denom


def _resid_body(*refs, gate_row, mod_rows, router):
    x_ref, y_ref, ga_ref, moda_ref = refs[:4]
    pos = 4
    xn = x_ref[...] + moda_ref[0, gate_row:gate_row + 1, :] * _rms(y_ref[...].astype(F32), ga_ref[...])
    if mod_rows is None:
        refs[pos][...] = xn
        return
    gb_ref, modb_ref = refs[pos], refs[pos + 1]
    pos += 2
    if router:
        wr_ref, br_ref = refs[pos], refs[pos + 1]
        pos += 2
    xn_ref, h_ref = refs[pos], refs[pos + 1]
    xn_ref[...] = xn
    sh, sc = mod_rows
    h = _rms(xn, gb_ref[...]) * (1.0 + modb_ref[0, sc:sc + 1, :]) + modb_ref[0, sh:sh + 1, :]
    h_ref[...] = h.astype(h_ref.dtype)
    if router:
        gates_ref = refs[pos + 2]
        logits = jnp.dot(h, wr_ref[...], precision=lax.Precision.HIGHEST,
                         preferred_element_type=F32) + br_ref[...]
        gates_ref[...] = _top4_gates(logits)


def _resid_norm(x, y, ga, mods_a, gate_row, s_lat, gb=None, mods_b=None, mod_rows=None, router=None):
    t, d = x.shape
    tm = EW_ROWS
    nb_lat = s_lat // tm
    row = pl.BlockSpec((tm, d), lambda i: (i, 0))
    vec = pl.BlockSpec((1, d), lambda i: (0, 0))
    mod = pl.BlockSpec((1, 6, d), lambda i: (jnp.where(i >= nb_lat, 1, 0), 0, 0))
    ins = [x, y, ga.reshape(1, d), mods_a]
    in_specs = [row, row, vec, mod]
    out_shape = [jax.ShapeDtypeStruct((t, d), F32)]
    out_specs = [row]
    if mod_rows is not None:
        ins += [gb.reshape(1, d), mods_b]
        in_specs += [vec, mod]
        out_shape.append(jax.ShapeDtypeStruct((t, d), BF16))
        out_specs.append(row)
        if router is not None:
            wr, br = router
            ins += [wr, br.reshape(1, N_EXPERTS)]
            in_specs += [pl.BlockSpec((d, N_EXPERTS), lambda i: (0, 0)),
                         pl.BlockSpec((1, N_EXPERTS), lambda i: (0, 0))]
            out_shape.append(jax.ShapeDtypeStruct((t, N_EXPERTS), F32))
            out_specs.append(pl.BlockSpec((tm, N_EXPERTS), lambda i: (i, 0)))
    body = functools.partial(_resid_body, gate_row=gate_row, mod_rows=mod_rows, router=router is not None)
    outs = pl.pallas_call(
        body, grid=(t // tm,), in_specs=in_specs, out_specs=out_specs, out_shape=out_shape,
        name="resid_norm", compiler_params=_params(("parallel",)),
    )(*ins)
    return outs


def _silu(x):
    return x * jax.nn.sigmoid(x)


def _adaln(cond8, a, b, bias):
    d = cond8.shape[1]
    sil = _silu(cond8)
    low = _matmul(sil, a.astype(BF16), out_dtype=F32, tm=8, name="ada_a")
    m = _matmul(low, b.astype(BF16), out_dtype=F32, tm=8, epi=_epi_bias,
                epi_ops=[_row_vec_op(bias, 1024)], tn=1024, name="ada_b")
    return m[:2].reshape(2, 6, d)


def _flash_body(q_ref, k_ref, v_ref, o_ref, *, scale, tk, n_main, tail):
    q = (q_ref[0].astype(F32) * scale).astype(BF16)
    tq = q.shape[0]
    dv = v_ref.shape[2]

    def step(carry, ks, vs):
        m, l, acc = carry
        s = lax.dot_general(q, ks, (((1,), (1,)), ((), ())), preferred_element_type=F32)
        m_new = jnp.maximum(m, jnp.max(s, axis=1, keepdims=True))
        alpha = jnp.exp(m - m_new)
        p = jnp.exp(s - m_new)
        l = alpha * l + jnp.sum(p, axis=1, keepdims=True)
        acc = alpha * acc + jnp.dot(p.astype(BF16), vs, preferred_element_type=F32)
        return m_new, l, acc

    def body(c, carry):
        off = pl.multiple_of(c * tk, tk)
        return step(carry, k_ref[0, pl.ds(off, tk), :], v_ref[0, pl.ds(off, tk), :])

    carry = (jnp.full((tq, 1), -jnp.inf, F32), jnp.zeros((tq, 1), F32), jnp.zeros((tq, dv), F32))
    carry = lax.fori_loop(0, n_main, body, carry)
    if tail:
        lo = n_main * tk
        carry = step(carry, k_ref[0, lo:lo + tail, :], v_ref[0, lo:lo + tail, :])
    _, l, acc = carry
    o_ref[...] = (acc / l).astype(o_ref.dtype)


def _flash(q, k, v, *, scale, q_rows, k_rows, tq, name):
    h, _, dk = q.shape
    dv = v.shape[2]
    q0, nq = q_rows
    k0, nk = k_rows
    assert q0 % tq == 0 and nq % tq == 0
    assert k0 % nk == 0
    tk = 1024 if nk >= 1024 else nk
    n_main = nk // tk
    tail = nk - n_main * tk
    assert tail % SUBLANES == 0
    body = functools.partial(_flash_body, scale=scale, tk=tk, n_main=n_main, tail=tail)
    return pl.pallas_call(
        body, grid=(h, nq // tq),
        in_specs=[pl.BlockSpec((1, tq, dk), lambda hh, i: (hh, q0 // tq + i, 0)),
                  pl.BlockSpec((1, nk, dk), lambda hh, i: (hh, k0 // nk, 0)),
                  pl.BlockSpec((1, nk, dv), lambda hh, i: (hh, k0 // nk, 0))],
        out_specs=pl.BlockSpec((tq, dv), lambda hh, i: (i, hh)),
        out_shape=jax.ShapeDtypeStruct((nq, h * dv), BF16), name=name,
        compiler_params=_params(("parallel", "arbitrary")),
    )(q, k, v)


def _mla_mixer(h, p, m, rope_c, rope_s, s_lat, with_ctx):
    t, d = h.shape
    n_ctx = t - s_lat
    hh = MLA_HEADS
    tm = _row_tile(t, MM_ROWS)
    cq = _matmul(h, p["mla_w_dq"][m].astype(BF16), out_dtype=BF16, tn=p["mla_w_dq"].shape[2],
                 epi=_epi_rmsnorm, epi_ops=[_row_vec_op(p["mla_q_norm"][m], p["mla_w_dq"].shape[2])], name="mla_dq")
    w_dkv = p["mla_w_dkv"][m]
    ckv = _matmul(h, w_dkv[:, :MLA_KV_RANK].astype(BF16), out_dtype=BF16, tn=MLA_KV_RANK,
                  epi=_epi_rmsnorm, epi_ops=[_row_vec_op(p["mla_kv_norm"][m], MLA_KV_RANK)], name="mla_dkv")
    w_pe = w_dkv[:, MLA_KV_RANK:].astype(BF16)
    kpe = _matmul(h, jnp.concatenate([w_pe, w_pe], axis=1), out_dtype=BF16, tn=LANES,
                  epi=_epi_rope, epi_ops=_rope_ops(rope_c, rope_s, tm), name="mla_kpe")[:, :MLA_ROPE_DIM]
    w_uq = p["mla_w_uq"][m].astype(BF16).reshape(-1, hh, MLA_QK_DIM)
    qn = _matmul(cq, w_uq[:, :, :MLA_NOPE_DIM].reshape(-1, hh * MLA_NOPE_DIM), out_dtype=BF16, name="mla_uq_nope")
    qr = _matmul(cq, w_uq[:, :, MLA_NOPE_DIM:].reshape(-1, hh * MLA_ROPE_DIM), out_dtype=BF16,
                 epi=_epi_rope, epi_ops=_rope_ops(rope_c, rope_s, tm), name="mla_uq_rope")
    kv = _matmul(ckv, p["mla_w_ukv"][m].astype(BF16), out_dtype=BF16, name="mla_ukv")
    kv3 = kv.reshape(t, hh, MLA_NOPE_DIM + MLA_V_DIM)
    q = jnp.concatenate([qn.reshape(t, hh, MLA_NOPE_DIM), qr.reshape(t, hh, MLA_ROPE_DIM)], axis=-1).transpose(1, 0, 2)
    k = jnp.concatenate([kv3[..., :MLA_NOPE_DIM], jnp.broadcast_to(kpe[:, None, :], (t, hh, MLA_ROPE_DIM))],
                        axis=-1).transpose(1, 0, 2)
    v = kv3[..., MLA_NOPE_DIM:].transpose(1, 0, 2)
    scale = MLA_QK_DIM ** -0.5
    tq = 512 if s_lat % 512 == 0 else SWA_BLOCK
    o = _flash(q, k, v, scale=scale, q_rows=(0, s_lat), k_rows=(0, t), tq=tq, name="mla_attn_lat")
    if with_ctx:
        o_c = _flash(q, k, v, scale=scale, q_rows=(s_lat, n_ctx), k_rows=(s_lat, n_ctx), tq=n_ctx, name="mla_attn_ctx")
        o = jnp.concatenate([o, o_c], axis=0)
    else:
        o = jnp.concatenate([o, jnp.zeros((n_ctx, o.shape[1]), o.dtype)], axis=0)
    return _matmul(o, p["mla_w_o"][m].astype(BF16), out_dtype=BF16, name="mla_o")


def _swa_body(sink_ref, q_ref, kp_ref, kc_ref, kn_ref, kx_ref, vp_ref, vc_ref, vn_ref, vx_ref, o_ref,
              *, scale, nb, grp, latent):
    j = pl.program_id(0)
    kh = pl.program_id(1)
    qb = q_ref[...]
    blk = qb.shape[0]
    dh = SWA_HEAD_DIM
    if latent:
        keys = jnp.concatenate([kx_ref[0], kp_ref[0], kc_ref[0], kn_ref[0]], axis=0)
        vals = jnp.concatenate([vx_ref[0], vp_ref[0], vc_ref[0], vn_ref[0]], axis=0)
        n_ctx = kx_ref.shape[1]
        nkeys = keys.shape[0]
        r = lax.broadcasted_iota(jnp.int32, (blk, nkeys), 0)
        c = lax.broadcasted_iota(jnp.int32, (blk, nkeys), 1) - n_ctx
        rel = c - blk - r
        kpos = (j - 1) * blk + c
        ok = (c < 0) | ((jnp.abs(rel) <= SWA_WINDOW) & (kpos >= 0) & (kpos < nb * blk))
    else:
        keys, vals = kx_ref[0], vx_ref[0]
        ok = None
    outs = []
    for g in range(grp):
        qg = (qb[:, g * dh:(g + 1) * dh].astype(F32) * scale).astype(BF16)
        s = lax.dot_general(qg, keys, (((1,), (1,)), ((), ())), preferred_element_type=F32)
        if ok is not None:
            s = jnp.where(ok, s, -jnp.inf)
        sink = sink_ref[kh * grp + g]
        m = jnp.maximum(jnp.max(s, axis=1, keepdims=True), sink)
        pr = jnp.exp(s - m)
        denom = jnp.sum(pr, axis=1, keepdims=True) + jnp.exp(sink - m)
        og = jnp.dot(pr.astype(BF16), vals, preferred_element_type=F32) / denom
        outs.append(og)
    o_ref[...] = jnp.concatenate(outs, axis=1).astype(o_ref.dtype)


def _swa_attention(q, k, v, sink, s_lat, latent):
    t = q.shape[0]
    n_ctx = t - s_lat
    grp = SWA_HEADS // SWA_KV_HEADS
    blk = SWA_BLOCK
    dh = SWA_HEAD_DIM
    width = grp * dh
    if latent:
        nb = s_lat // blk
        q0 = 0
    else:
        nb = 1
        blk = n_ctx
        q0 = s_lat // n_ctx
    ctx_blk = s_lat // n_ctx

    def kv_spec(shift):
        return pl.BlockSpec((1, SWA_BLOCK, dh), lambda j, kh: (kh, jnp.clip(j + shift, 0, max(nb - 1, 0)), 0))

    ctx_spec = pl.BlockSpec((1, n_ctx, dh), lambda j, kh: (kh, ctx_blk, 0))
    body = functools.partial(_swa_body, scale=dh ** -0.5, nb=nb, grp=grp, latent=latent)
    win = [kv_spec(-1), kv_spec(0), kv_spec(1)]
    return pl.pallas_call(
        body, grid=(nb, SWA_KV_HEADS),
        in_specs=[pl.BlockSpec(memory_space=pltpu.SMEM),
                  pl.BlockSpec((blk, width), lambda j, kh: (q0 + j, kh))] + win + [ctx_spec] + win + [ctx_spec],
        out_specs=pl.BlockSpec((blk, width), lambda j, kh: (j, kh)),
        out_shape=jax.ShapeDtypeStruct((nb * blk, SWA_HEADS * dh), BF16),
        name="swa_attn_lat" if latent else "swa_attn_ctx",
        compiler_params=_params(("parallel", "parallel")),
    )(sink.astype(F32), q, k, k, k, k, v, v, v, v)


def _swa_mixer(h, p, m, rope_c, rope_s, s_lat, with_ctx):
    t, d = h.shape
    n_ctx = t - s_lat
    tm = _row_tile(t, MM_ROWS)
    qw = SWA_HEADS * SWA_HEAD_DIM
    kw = SWA_KV_HEADS * SWA_HEAD_DIM
    w = p["swa_w_qkv"][m].astype(BF16)
    b = p["swa_b_qkv"][m]
    tn = 512
    qk = _matmul(h, w[:, :qw + kw], out_dtype=BF16, tn=tn, epi=_epi_bias_rope,
                 epi_ops=[_row_vec_op(b[:qw + kw], tn)] + _rope_ops(rope_c, rope_s, tm), name="swa_qk")
    vv = _matmul(h, w[:, qw + kw:], out_dtype=BF16, tn=tn, epi=_epi_bias,
                 epi_ops=[_row_vec_op(b[qw + kw:], tn)], name="swa_v")
    q = qk[:, :qw]
    k = qk[:, qw:].reshape(t, SWA_KV_HEADS, SWA_HEAD_DIM).transpose(1, 0, 2)
    v = vv.reshape(t, SWA_KV_HEADS, SWA_HEAD_DIM).transpose(1, 0, 2)
    sink = p["swa_sink"][m]
    o = _swa_attention(q, k, v, sink, s_lat, True)
    if with_ctx:
        o = jnp.concatenate([o, _swa_attention(q, k, v, sink, s_lat, False)], axis=0)
    else:
        o = jnp.concatenate([o, jnp.zeros((n_ctx, o.shape[1]), o.dtype)], axis=0)
    return _matmul(o, p["swa_w_o"][m].astype(BF16), out_dtype=BF16, epi=_epi_bias,
                   epi_ops=[_row_vec_op(p["swa_b_o"][m], 1024)], tn=1024, name="swa_o")


def _perm_cols(w):
    lead = w.shape[:-1]
    nh = w.shape[-1] // RWKV_HEAD
    return jnp.swapaxes(w.reshape(lead + (nh, RWKV_HEAD)), -1, -2).reshape(lead + (nh * RWKV_HEAD,))


def _rwkv_mix_body(h_ref, hp_ref, hn_ref, mu_ref, *out_refs, nb_lat, nb):
    i = pl.program_id(0)
    h = h_ref[...].astype(F32)
    tm = h.shape[0]
    first = (i == 0) | (i == nb_lat)
    last = (i == nb_lat - 1) | (i == nb - 1)
    prev_row = jnp.where(first, 0.0, hp_ref[SUBLANES - 1:SUBLANES, :].astype(F32))
    next_row = jnp.where(last, 0.0, hn_ref[0:1, :].astype(F32))
    rid = lax.broadcasted_iota(jnp.int32, (tm, 1), 0)
    prev = jnp.where(rid == 0, prev_row, pltpu.roll(h, 1, 0))
    nxt = jnp.where(rid == tm - 1, next_row, pltpu.roll(h, tm - 1, 0))
    xx = 0.5 * (prev + nxt) - h
    for n, o_ref in enumerate(out_refs):
        o_ref[...] = (h + xx * mu_ref[n:n + 1, :]).astype(o_ref.dtype)


def _rwkv_mix(h, mu, s_lat):
    t, d = h.shape
    tm = 128
    nb, nb_lat = t // tm, s_lat // tm
    r8 = tm // SUBLANES
    body = functools.partial(_rwkv_mix_body, nb_lat=nb_lat, nb=nb)
    row = pl.BlockSpec((tm, d), lambda i: (i, 0))
    return pl.pallas_call(
        body, grid=(nb,),
        in_specs=[row,
                  pl.BlockSpec((SUBLANES, d), lambda i: (jnp.maximum(i * r8 - 1, 0), 0)),
                  pl.BlockSpec((SUBLANES, d), lambda i: (jnp.minimum((i + 1) * r8, t // SUBLANES - 1), 0)),
                  pl.BlockSpec((6, d), lambda i: (0, 0))],
        out_specs=[row] * 6, out_shape=[jax.ShapeDtypeStruct((t, d), BF16)] * 6, name="rwkv_mix",
        compiler_params=_params(("parallel",)),
    )(h, h, h, mu)


def _head_sum(ref_or_val_slabs):
    acc = ref_or_val_slabs[0]
    for s in ref_or_val_slabs[1:]:
        acc = acc + s
    return acc + pltpu.roll(acc, 64, 1)


def _rwkv_prep_body(r_ref, k_ref, v_ref, wl0_ref, wl1_ref, al0_ref, al1_ref, kk_ref, ka_ref, rk_ref,
                    d0_ref, d1_ref, kz0_ref, kz1_ref, b0_ref, b1_ref, na_ref, bonus_ref):
    nsl = r_ref.shape[1] // LANES
    sl = lambda ref, j: ref[:, j * LANES:(j + 1) * LANES]
    kks = [sl(k_ref, j) * sl(kk_ref, j) for j in range(nsl)]
    ss = _head_sum([x * x for x in kks])
    inv = lax.rsqrt(jnp.maximum(ss, 1e-24))
    bon = [None, None]
    kz_all = [[], []]
    for z, (wl_ref, al_ref, d_ref, kz_ref, b_ref) in enumerate(
            ((wl0_ref, al0_ref, d0_ref, kz0_ref, b0_ref), (wl1_ref, al1_ref, d1_ref, kz1_ref, b1_ref))):
        prods = []
        for j in range(nsl):
            kkn = kks[j] * inv
            if z == 0:
                na_ref[:, j * LANES:(j + 1) * LANES] = -kkn
            logw = -jax.nn.softplus(-sl(wl_ref, j)) - 0.5
            d_ref[:, j * LANES:(j + 1) * LANES] = jnp.exp(-jnp.exp(logw))
            a = jax.nn.sigmoid(sl(al_ref, j))
            kz = sl(k_ref, j) * (1.0 + (a - 1.0) * sl(ka_ref, j))
            kz_ref[:, j * LANES:(j + 1) * LANES] = kz
            b_ref[:, j * LANES:(j + 1) * LANES] = kkn * a
            prods.append(sl(r_ref, j) * kz * sl(rk_ref, j))
        bon[z] = _head_sum(prods)
    tot = bon[0] + bon[1]
    for j in range(nsl):
        bonus_ref[:, j * LANES:(j + 1) * LANES] = tot * sl(v_ref, j)


def _rwkv_prep(r, k, v, wl, al, k_k, k_a, r_k):
    t, d = r.shape
    tm = 64
    row = pl.BlockSpec((tm, d), lambda i: (i, 0))
    vec = pl.BlockSpec((1, d), lambda i: (0, 0))
    return pl.pallas_call(
        _rwkv_prep_body, grid=(t // tm,),
        in_specs=[row] * 7 + [vec] * 3,
        out_specs=[row] * 8, out_shape=[jax.ShapeDtypeStruct((t, d), F32)] * 8, name="rwkv_prep",
        compiler_params=_params(("parallel",)),
    )(r, k, v, wl[0], wl[1], al[0], al[1], k_k.reshape(1, d), k_a.reshape(1, d), r_k.reshape(1, d))


def _wkv_body(r_ref, w_ref, k_ref, a_ref, b_ref, v_ref, y_ref, s_ref, *, tb, reverse):
    @pl.when(pl.program_id(0) == 0)
    def _():
        s_ref[...] = jnp.zeros_like(s_ref)

    npair = s_ref.shape[0]

    def step(i, carry):
        t = (tb - 1 - i) if reverse else i
        dup = lambda ref: jnp.concatenate([ref[t], ref[t]], axis=1)
        a2, w2, b2, k2, r2 = dup(a_ref), dup(w_ref), dup(b_ref), dup(k_ref), dup(r_ref)
        for j in range(npair):
            st = s_ref[j]
            vrow = v_ref[t, pl.ds(j, 1), :]
            sa = jnp.sum(st * a2, axis=0, keepdims=True)
            sn = st * w2 + sa * b2 + vrow * k2
            s_ref[j] = sn
            y_ref[t, pl.ds(j, 1), :] = jnp.sum(sn * r2, axis=0, keepdims=True)
        return carry

    lax.fori_loop(0, tb, step, 0)


def _wkv_scan(r, w, k, a, b, v, s_lat, reverse):
    t, d = r.shape
    n = RWKV_HEAD
    nh = d // n
    tb = 64
    nb, nb_lat = t // tb, s_lat // tb
    nb_ctx = nb - nb_lat
    if reverse:
        order = lambda i: nb - 1 - i
    else:
        order = lambda i: jnp.where(i < nb_ctx, i + nb_lat, i - nb_ctx)
    kspec = pl.BlockSpec((tb, n, nh), lambda i: (order(i), 0, 0))
    vspec = pl.BlockSpec((tb, n // 2, 2 * nh), lambda i: (order(i), 0, 0))
    as3 = lambda z: z.reshape(t, n, nh)
    body = functools.partial(_wkv_body, tb=tb, reverse=reverse)
    y = pl.pallas_call(
        body, grid=(nb,),
        in_specs=[kspec] * 5 + [vspec], out_specs=vspec,
        out_shape=jax.ShapeDtypeStruct((t, n // 2, 2 * nh), F32),
        scratch_shapes=[pltpu.VMEM((n // 2, n, 2 * nh), F32)],
        name="wkv_rev" if reverse else "wkv_fwd",
        compiler_params=_params(("arbitrary",)),
    )(as3(r), as3(w), as3(k), as3(a), as3(b), v.reshape(t, n // 2, 2 * nh))
    return y.reshape(t, d)


def _rwkv_finish_body(y0_ref, y1_ref, bonus_ref, g_ref, lw_ref, lb_ref, o_ref):
    nsl = y0_ref.shape[1] // LANES
    ys = [y0_ref[:, j * LANES:(j + 1) * LANES] + y1_ref[:, j * LANES:(j + 1) * LANES] for j in range(nsl)]
    mean = _head_sum(ys) * (1.0 / RWKV_HEAD)
    cs = [y - mean for y in ys]
    var = _head_sum([c * c for c in cs]) * (1.0 / RWKV_HEAD)
    inv = lax.rsqrt(var + RWKV_GN_EPS)
    for j in range(nsl):
        s = slice(j * LANES, (j + 1) * LANES)
        out = cs[j] * inv * lw_ref[:, s] + lb_ref[:, s] + bonus_ref[:, s]
        o_ref[:, s] = (out * g_ref[:, s].astype(F32)).astype(o_ref.dtype)


def _rwkv_finish(y0, y1, bonus, g, ln_w, ln_b):
    t, d = y0.shape
    tm = 128
    row = pl.BlockSpec((tm, d), lambda i: (i, 0))
    vec = pl.BlockSpec((1, d), lambda i: (0, 0))
    return pl.pallas_call(
        _rwkv_finish_body, grid=(t // tm,), in_specs=[row] * 4 + [vec] * 2, out_specs=row,
        out_shape=jax.ShapeDtypeStruct((t, d), BF16), name="rwkv_finish",
        compiler_params=_params(("parallel",)),
    )(y0, y1, bonus, g, ln_w.reshape(1, d), ln_b.reshape(1, d))


def _rwkv_mixer(h, p, m, s_lat):
    t, d = h.shape
    mixes = _rwkv_mix(h, p["rwkv_mu"][m], s_lat)
    w_rkv = _perm_cols(p["rwkv_w_rkv"][m]).astype(BF16)
    r = _matmul(mixes[0], w_rkv[0], out_dtype=F32, name="rwkv_r")
    k = _matmul(mixes[1], w_rkv[1], out_dtype=F32, name="rwkv_k")
    v = _matmul(mixes[2], w_rkv[2], out_dtype=F32, name="rwkv_v")
    lora = p["rwkv_w1"].shape[-1]
    w1 = jnp.concatenate([p["rwkv_w1"][m, 0], p["rwkv_w1"][m, 1]], axis=1).astype(BF16)
    a1 = jnp.concatenate([p["rwkv_a1"][m, 0], p["rwkv_a1"][m, 1]], axis=1).astype(BF16)
    wmid = _matmul(mixes[3], w1, out_dtype=BF16, epi=functools.partial(_epi_act, act=jnp.tanh), name="rwkv_w1")
    amid = _matmul(mixes[4], a1, out_dtype=BF16, name="rwkv_a1")
    w2 = _perm_cols(p["rwkv_w2"][m]).astype(BF16)
    a2 = _perm_cols(p["rwkv_a2"][m]).astype(BF16)
    w0 = _perm_cols(p["rwkv_w0"][m])
    a0 = _perm_cols(p["rwkv_a0"][m])
    wl, al = [], []
    for z in range(2):
        wl.append(_matmul(wmid[:, z * lora:(z + 1) * lora], w2[z], out_dtype=F32, epi=_epi_bias,
                          epi_ops=[_row_vec_op(w0[z], 1024)], tn=1024, name="rwkv_w2"))
        al.append(_matmul(amid[:, z * lora:(z + 1) * lora], a2[z], out_dtype=F32, epi=_epi_bias,
                          epi_ops=[_row_vec_op(a0[z], 1024)], tn=1024, name="rwkv_a2"))
    gmid = _matmul(mixes[5], p["rwkv_g1"][m].astype(BF16), out_dtype=BF16,
                   epi=functools.partial(_epi_act, act=jax.nn.sigmoid), name="rwkv_g1")
    g = _matmul(gmid, _perm_cols(p["rwkv_g2"][m]).astype(BF16), out_dtype=BF16, name="rwkv_g2")
    d0, d1, kz0, kz1, b0, b1, na, bonus = _rwkv_prep(
        r, k, v, wl, al, _perm_cols(p["rwkv_k_k"][m]), _perm_cols(p["rwkv_k_a"][m]), _perm_cols(p["rwkv_r_k"][m]))
    y0 = _wkv_scan(r, d0, kz0, na, b0, v, s_lat, False)
    y1 = _wkv_scan(r, d1, kz1, na, b1, v, s_lat, True)
    out = _rwkv_finish(y0, y1, bonus, g, _perm_cols(p["rwkv_ln_w"][m]), _perm_cols(p["rwkv_ln_b"][m]))
    w_o = p["rwkv_w_o"][m]
    nh = d // RWKV_HEAD
    w_o = w_o.reshape(nh, RWKV_HEAD, d).transpose(1, 0, 2).reshape(d, d).astype(BF16)
    return _matmul(out, w_o, out_dtype=BF16, name="rwkv_o")


def _epi_moe_act(acc, ids, b1_ref, gates_ref):
    e = ids[1]
    u = acc + b1_ref[0]
    x_glu = jnp.minimum(u[:, :EXPERT_FF], SWIGLU_LIMIT)
    x_lin = jnp.clip(u[:, EXPERT_FF:], -SWIGLU_LIMIT, SWIGLU_LIMIT)
    act = x_glu * jax.nn.sigmoid(SWIGLU_ALPHA * x_glu) * (x_lin + 1.0)
    gates = gates_ref[...]
    lane = lax.broadcasted_iota(jnp.int32, gates.shape, 1)
    gcol = jnp.sum(jnp.where(lane == e, gates, 0.0), axis=1, keepdims=True)
    return act * gcol


def _epi_moe_out(acc, ids, gates_ref, b2_ref):
    return acc + jnp.dot(gates_ref[...], b2_ref[...], preferred_element_type=F32)


def _moe(h2, gates, w1, b1, w2, b2):
    t, d = h2.shape
    e, _, f2 = w1.shape
    ff = f2 // 2
    tm = _row_tile(t, MM_ROWS)
    gate_spec = pl.BlockSpec((tm, e), lambda i, j, k: (i, 0))
    act = _matmul(
        h2, w1.astype(BF16), out_dtype=BF16, tm=tm, tn=f2, tk=d, n_total=e * f2,
        w_spec=pl.BlockSpec((None, d, f2), lambda i, j, k: (j, 0, 0)),
        epi=_epi_moe_act,
        epi_ops=[(b1.reshape(e, 1, f2), pl.BlockSpec((1, 1, f2), lambda i, j, k: (j, 0, 0))), (gates, gate_spec)],
        out_shape=jax.ShapeDtypeStruct((t, e * ff), BF16),
        out_spec=pl.BlockSpec((tm, ff), lambda i, j, k: (i, j)), name="moe_up")
    tn = 1024
    return _matmul(
        act, w2.astype(BF16).reshape(e * ff, d), out_dtype=BF16, tm=tm, tn=tn, epi=_epi_moe_out,
        epi_ops=[(gates, gate_spec), (b2, pl.BlockSpec((e, tn), lambda i, j, k: (0, j)))], name="moe_down")


def kernel(x, c, ctx, c_ctx, ada_a, ada_b, ada_bias, norm_g, mla_w_dq, mla_q_norm, mla_w_uq, mla_w_dkv, mla_kv_norm, mla_w_ukv, mla_w_o, rwkv_mu, rwkv_w_rkv, rwkv_w0, rwkv_w1, rwkv_w2, rwkv_a0, rwkv_a1, rwkv_a2, rwkv_g1, rwkv_g2, rwkv_k_k, rwkv_k_a, rwkv_r_k, rwkv_ln_w, rwkv_ln_b, rwkv_w_o, swa_w_qkv, swa_b_qkv, swa_sink, swa_w_o, swa_b_o, moe_w_router, moe_b_router, moe_w1, moe_b1, moe_w2, moe_b2):
    p = dict(mla_w_dq=mla_w_dq, mla_q_norm=mla_q_norm, mla_w_uq=mla_w_uq, mla_w_dkv=mla_w_dkv,
             mla_kv_norm=mla_kv_norm, mla_w_ukv=mla_w_ukv, mla_w_o=mla_w_o,
             rwkv_mu=rwkv_mu, rwkv_w_rkv=rwkv_w_rkv, rwkv_w0=rwkv_w0, rwkv_w1=rwkv_w1, rwkv_w2=rwkv_w2,
             rwkv_a0=rwkv_a0, rwkv_a1=rwkv_a1, rwkv_a2=rwkv_a2, rwkv_g1=rwkv_g1, rwkv_g2=rwkv_g2,
             rwkv_k_k=rwkv_k_k, rwkv_k_a=rwkv_k_a, rwkv_r_k=rwkv_r_k, rwkv_ln_w=rwkv_ln_w,
             rwkv_ln_b=rwkv_ln_b, rwkv_w_o=rwkv_w_o, swa_w_qkv=swa_w_qkv, swa_b_qkv=swa_b_qkv,
             swa_sink=swa_sink, swa_w_o=swa_w_o, swa_b_o=swa_b_o)
    batch, s_lat, d = x.shape
    n_ctx = ctx.shape[1]
    assert batch == 1 and s_lat % n_ctx == 0 and n_ctx % EW_ROWS == 0
    depth = ada_a.shape[0]
    xs = jnp.concatenate([x[0], ctx[0]], axis=0)
    cond8 = jnp.concatenate([c, c_ctx[None, :], jnp.zeros((SUBLANES - 2, d), F32)], axis=0)
    rope_c, rope_s = _rope_tables(s_lat, n_ctx)
    mods = [_adaln(cond8, ada_a[i], ada_b[i], ada_bias[i]) for i in range(depth)]
    h = _norm_mod(xs, norm_g[0, 0], mods[0], s_lat)
    for i in range(depth):
        last = i == depth - 1
        kind, m = i % N_MIXERS, i // N_MIXERS
        if kind == 0:
            y = _mla_mixer(h, p, m, rope_c, rope_s, s_lat, not last)
        elif kind == 1:
            y = _rwkv_mixer(h, p, m, s_lat)
        else:
            y = _swa_mixer(h, p, m, rope_c, rope_s, s_lat, not last)
        xs, h2, gates = _resid_norm(xs, y, norm_g[i, 1], mods[i], 2, s_lat, gb=norm_g[i, 2], mods_b=mods[i],
                                    mod_rows=(3, 4), router=(moe_w_router[i], moe_b_router[i]))
        f = _moe(h2, gates, moe_w1[i], moe_b1[i], moe_w2[i], moe_b2[i])
        if last:
            (xs,) = _resid_norm(xs, f, norm_g[i, 3], mods[i], 5, s_lat)
        else:
            xs, h = _resid_norm(xs, f, norm_g[i, 3], mods[i], 5, s_lat, gb=norm_g[i + 1, 0],
                                mods_b=mods[i + 1], mod_rows=(0, 1))
    return xs[:s_lat][None]
```

```python
import functools

import jax
import jax.numpy as jnp
import numpy as np
from jax import lax
from jax.experimental import pallas as pl
from jax.experimental.pallas import tpu as pltpu

F32 = jnp.float32
BF16 = jnp.bfloat16

GRID_W = 64
ROPE_BASE = 10000.0
NORM_EPS = 1e-6
MLA_HEADS = 64
MLA_KV_RANK = 512
MLA_NOPE_DIM = 128
MLA_ROPE_DIM = 64
MLA_V_DIM = 128
MLA_QK_DIM = MLA_NOPE_DIM + MLA_ROPE_DIM
RWKV_HEAD = 64
RWKV_GN_EPS = 64e-5
SWA_HEADS = 64
SWA_KV_HEADS = 8
SWA_HEAD_DIM = 64
SWA_WINDOW = 128
SWA_BLOCK = 128
N_EXPERTS = 32
TOP_K = 4
EXPERT_FF = 384
SWIGLU_ALPHA = 1.702
SWIGLU_LIMIT = 7.0
N_MIXERS = 3

LANES = 128
SUBLANES = 8
VMEM_LIMIT = 56 * 1024 * 1024
LOG2E = 1.4426950408889634

MM_ROWS = 1280
EW_ROWS = 256


def _params(sem):
    return pltpu.CompilerParams(dimension_semantics=sem, vmem_limit_bytes=VMEM_LIMIT)


def _row_tile(t, pref):
    if t <= pref:
        return t
    best = None
    for cand in range(pref, 7, -8):
        if t % cand == 0:
            best = cand
            break
    assert best is not None, (t, pref)
    return best


def _col_tile(n, pref):
    if n <= pref:
        return n
    for cand in range(pref, LANES - 1, -LANES):
        if n % cand == 0:
            return cand
    return n


def _mm_body(*refs, nk, n_epi, epi, store):
    x_ref, w_ref = refs[0], refs[1]
    epi_refs = refs[2:2 + n_epi]
    o_ref = refs[2 + n_epi]
    ids = (pl.program_id(0), pl.program_id(1))
    part = jnp.dot(x_ref[...].astype(BF16), w_ref[...], preferred_element_type=F32)
    if nk == 1:
        store(o_ref, epi(part, ids, *epi_refs))
        return
    acc_ref = refs[3 + n_epi]
    k = pl.program_id(2)

    @pl.when(k == 0)
    def _():
        acc_ref[...] = part

    @pl.when(k > 0)
    def _():
        acc_ref[...] += part

    @pl.when(k == nk - 1)
    def _():
        store(o_ref, epi(acc_ref[...], ids, *epi_refs))


def _store_plain(o_ref, val):
    o_ref[...] = val.astype(o_ref.dtype)


def _epi_none(acc, ids):
    return acc


def _matmul(x, w, *, out_dtype, tm=None, tn=None, tk=None, epi=_epi_none, epi_ops=(),
            w_spec=None, n_total=None, out_shape=None, out_spec=None, store=_store_plain, name="mm"):
    m, kdim = x.shape
    n = n_total if n_total is not None else w.shape[1]
    tm = tm or _row_tile(m, MM_ROWS)
    tn = tn or _col_tile(n, 1024)
    tk = tk or (kdim if kdim <= 1024 else _col_tile(kdim, 1024))
    assert m % tm == 0 and n % tn == 0 and kdim % tk == 0, (m, n, kdim, tm, tn, tk)
    nk = kdim // tk
    grid = (m // tm, n // tn, nk)
    if w_spec is None:
        w_spec = pl.BlockSpec((tk, tn), lambda i, j, k: (k, j))
    in_specs = [pl.BlockSpec((tm, tk), lambda i, j, k: (i, k)), w_spec] + [s for _, s in epi_ops]
    if out_shape is None:
        out_shape = jax.ShapeDtypeStruct((m, n), out_dtype)
        out_spec = pl.BlockSpec((tm, tn), lambda i, j, k: (i, j))
    scratch = [pltpu.VMEM((tm, tn), F32)] if nk > 1 else []
    body = functools.partial(_mm_body, nk=nk, n_epi=len(epi_ops), epi=epi, store=store)
    return pl.pallas_call(
        body, grid=grid, in_specs=in_specs, out_specs=out_spec, out_shape=out_shape,
        scratch_shapes=scratch, name=name,
        compiler_params=_params(("parallel", "parallel", "arbitrary")),
    )(x, w, *[a for a, _ in epi_ops])


def _row_vec_op(vec, tn):
    return (vec.reshape(1, -1).astype(F32), pl.BlockSpec((1, tn), lambda i, j, k: (0, j)))


def _epi_bias(acc, ids, b_ref):
    return acc + b_ref[...]


def _epi_bias_act(acc, ids, b_ref, *, act):
    return act(acc + b_ref[...])


def _epi_act(acc, ids, *, act):
    return act(acc)


def _epi_rmsnorm(acc, ids, g_ref):
    ms = jnp.mean(acc * acc, axis=-1, keepdims=True)
    return acc * lax.rsqrt(ms + NORM_EPS) * g_ref[...]


def _rope_cols(x, c, sg):
    lane = lax.broadcasted_iota(jnp.int32, (1, LANES), 1)
    first_half = (lane % 64) < 32
    outs = []
    for g in range(x.shape[1] // LANES):
        xg = x[:, g * LANES:(g + 1) * LANES]
        partner = jnp.where(first_half, pltpu.roll(xg, LANES - 32, 1), pltpu.roll(xg, 32, 1))
        outs.append(xg * c + partner * sg)
    return outs[0] if len(outs) == 1 else jnp.concatenate(outs, axis=1)


def _epi_rope(acc, ids, c_ref, s_ref):
    return _rope_cols(acc, c_ref[...], s_ref[...])


def _epi_bias_rope(acc, ids, b_ref, c_ref, s_ref):
    return _rope_cols(acc + b_ref[...], c_ref[...], s_ref[...])


def _rope_ops(rope_c, rope_s, tm):
    spec = pl.BlockSpec((tm, LANES), lambda i, j, k: (i, 0))
    return [(rope_c, spec), (rope_s, spec)]


def _rope_tables(s_lat, n_ctx):
    t = jnp.arange(s_lat)
    row = (t // GRID_W).astype(F32)
    col = (t % GRID_W).astype(F32)
    axis_dim = 32
    inv = ROPE_BASE ** (-jnp.arange(0, axis_dim, 2, dtype=F32) / axis_dim)
    ang = jnp.concatenate([row[:, None] * inv, col[:, None] * inv], axis=-1)
    cos, sin = jnp.cos(ang), jnp.sin(ang)
    c64 = jnp.concatenate([cos, cos], axis=-1)
    s64 = jnp.concatenate([-sin, sin], axis=-1)
    c = jnp.concatenate([c64, c64], axis=-1)
    s = jnp.concatenate([s64, s64], axis=-1)
    c = jnp.concatenate([c, jnp.ones((n_ctx, LANES), F32)], axis=0)
    s = jnp.concatenate([s, jnp.zeros((n_ctx, LANES), F32)], axis=0)
    return c, s


def _rms(x, g):
    return x * lax.rsqrt(jnp.mean(x * x, axis=-1, keepdims=True) + NORM_EPS) * g


def _norm_mod_body(x_ref, g_ref, mod_ref, h_ref):
    h = _rms(x_ref[...], g_ref[...]) * (1.0 + mod_ref[0, 1:2, :]) + mod_ref[0, 0:1, :]
    h_ref[...] = h.astype(h_ref.dtype)


def _norm_mod(x, g, mods, s_lat):
    t, d = x.shape
    tm = EW_ROWS
    nb_lat = s_lat // tm
    return pl.pallas_call(
        _norm_mod_body, grid=(t // tm,),
        in_specs=[pl.BlockSpec((tm, d), lambda i: (i, 0)),
                  pl.BlockSpec((1, d), lambda i: (0, 0)),
                  pl.BlockSpec((1, 6, d), lambda i: (jnp.where(i >= nb_lat, 1, 0), 0, 0))],
        out_specs=pl.BlockSpec((tm, d), lambda i: (i, 0)),
        out_shape=jax.ShapeDtypeStruct((t, d), BF16), name="norm_mod",
        compiler_params=_params(("parallel",)),
    )(x, g.reshape(1, d), mods)


def _top4_gates(logits):
    lane = lax.broadcasted_iota(jnp.int32, logits.shape, 1)
    work = logits
    gate = jnp.zeros_like(logits)
    denom = jnp.zeros((logits.shape[0], 1), F32)
    m0 = None
    for _ in range(TOP_K):
        m = jnp.max(work, axis=1, keepdims=True)
        idx = jnp.min(jnp.where(work == m, lane, N_EXPERTS), axis=1, keepdims=True)
        hit = lane == idx
        if m0 is None:
            m0 = m
        wk = jnp.exp(m - m0)
        gate = gate + jnp.where(hit, wk, 0.0)
        denom = denom + wk
        work = jnp.where(hit, -jnp.inf, work)
    return gate / denom


def _resid_body(*refs, gate_row, mod_rows, router):
    x_ref, y_ref, ga_ref, moda_ref = refs[:4]
    pos = 4
    xn = x_ref[...] + moda_ref[0, gate_row:gate_row + 1, :] * _rms(y_ref[...].astype(F32), ga_ref[...])
    if mod_rows is None:
        refs[pos][...] = xn
        return
    gb_ref, modb_ref = refs[pos], refs[pos + 1]
    pos += 2
    if router:
        wr_ref, br_ref = refs[pos], refs[pos + 1]
        pos += 2
    xn_ref, h_ref = refs[pos], refs[pos + 1]
    xn_ref[...] = xn
    sh, sc = mod_rows
    h = _rms(xn, gb_ref[...]) * (1.0 + modb_ref[0, sc:sc + 1, :]) + modb_ref[0, sh:sh + 1, :]
    h_ref[...] = h.astype(h_ref.dtype)
    if router:
        gates_ref = refs[pos + 2]
        logits = jnp.dot(h, wr_ref[...], precision=lax.Precision.HIGHEST,
                         preferred_element_type=F32) + br_ref[...]
        gates_ref[...] = _top4_gates(logits)


def _resid_norm(x, y, ga, mods_a, gate_row, s_lat, gb=None, mods_b=None, mod_rows=None, router=None):
    t, d = x.shape
    tm = EW_ROWS
    nb_lat = s_lat // tm
    row = pl.BlockSpec((tm, d), lambda i: (i, 0))
    vec = pl.BlockSpec((1, d), lambda i: (0, 0))
    mod = pl.BlockSpec((1, 6, d), lambda i: (jnp.where(i >= nb_lat, 1, 0), 0, 0))
    ins = [x, y, ga.reshape(1, d), mods_a]
    in_specs = [row, row, vec, mod]
    out_shape = [jax.ShapeDtypeStruct((t, d), F32)]
    out_specs = [row]
    if mod_rows is not None:
        ins += [gb.reshape(1, d), mods_b]
        in_specs += [vec, mod]
        out_shape.append(jax.ShapeDtypeStruct((t, d), BF16))
        out_specs.append(row)
        if router is not None:
            wr, br = router
            ins += [wr, br.reshape(1, N_EXPERTS)]
            in_specs += [pl.BlockSpec((d, N_EXPERTS), lambda i: (0, 0)),
                         pl.BlockSpec((1, N_EXPERTS), lambda i: (0, 0))]
            out_shape.append(jax.ShapeDtypeStruct((t, N_EXPERTS), F32))
            out_specs.append(pl.BlockSpec((tm, N_EXPERTS), lambda i: (i, 0)))
    body = functools.partial(_resid_body, gate_row=gate_row, mod_rows=mod_rows, router=router is not None)
    outs = pl.pallas_call(
        body, grid=(t // tm,), in_specs=in_specs, out_specs=out_specs, out_shape=out_shape,
        name="resid_norm", compiler_params=_params(("parallel",)),
    )(*ins)
    return outs


def _silu(x):
    return x * jax.nn.sigmoid(x)


def _adaln(cond8, a, b, bias):
    d = cond8.shape[1]
    sil = _silu(cond8)
    low = _matmul(sil, a.astype(BF16), out_dtype=F32, tm=8, name="ada_a")
    m = _matmul(low, b.astype(BF16), out_dtype=F32, tm=8, epi=_epi_bias,
                epi_ops=[_row_vec_op(bias, 1024)], tn=1024, name="ada_b")
    return m[:2].reshape(2, 6, d)


def _mla_attn_body(qn_ref, qr_ref, kn_ref, kpe_ref, v_ref, o_ref, k_scr, vt_scr, *, scale, tk, n_main, tail):
    hd = pl.program_id(0)
    nkeys = k_scr.shape[0]
    vchunk = 256

    @pl.when(pl.program_id(1) == 0)
    def _():
        k_scr[:, :MLA_NOPE_DIM] = kn_ref[...]
        k_scr[:, MLA_NOPE_DIM:] = kpe_ref[:, :MLA_ROPE_DIM]

        def tr(c, carry):
            off = pl.multiple_of(c * vchunk, vchunk)
            vt_scr[:, pl.ds(off, vchunk)] = v_ref[pl.ds(off, vchunk), :].astype(F32).T.astype(BF16)
            return carry

        lax.fori_loop(0, nkeys // vchunk, tr, 0)

    c0 = scale * LOG2E
    qn_t = qn_ref[...].astype(F32).T * c0
    qr_t = qr_ref[...].astype(F32).T * c0
    qr_t = jnp.where(hd % 2 == 0, qr_t[:MLA_ROPE_DIM], qr_t[MLA_ROPE_DIM:])
    qt = jnp.concatenate([qn_t, qr_t], axis=0).astype(BF16)
    tq = qt.shape[1]
    dv = vt_scr.shape[0]

    def step(carry, ks, vts):
        m, l, acc = carry
        s = jnp.dot(ks, qt, preferred_element_type=F32)
        m_new = jnp.maximum(m, jnp.max(s, axis=0, keepdims=True))
        alpha = jnp.exp2(m - m_new)
        p = jnp.exp2(s - m_new)
        l = alpha * l + jnp.sum(p, axis=0, keepdims=True)
        acc = alpha * acc + jnp.dot(vts, p.astype(BF16), preferred_element_type=F32)
        return m_new, l, acc

    def body(c, carry):
        off = pl.multiple_of(c * tk, tk)
        return step(carry, k_scr[pl.ds(off, tk), :], vt_scr[:, pl.ds(off, tk)])

    carry = (jnp.full((1, tq), -jnp.inf, F32), jnp.zeros((1, tq), F32), jnp.zeros((dv, tq), F32))
    if n_main:
        carry = lax.fori_loop(0, n_main, body, carry)
    if tail:
        lo = n_main * tk
        carry = step(carry, k_scr[lo:lo + tail, :], vt_scr[:, lo:lo + tail])
    _, l, acc = carry
    o_ref[...] = (acc / l).T.astype(o_ref.dtype)


def _mla_attention(qn, qr, kv, kpe, *, q_rows, k_rows, tq, name):
    hh = MLA_HEADS
    q0, nq = q_rows
    k0, nkeys = k_rows
    assert q0 % tq == 0 and nq % tq == 0 and k0 % nkeys == 0 and nkeys % 256 == 0
    tk = min(1024, nkeys)
    n_main = nkeys // tk
    tail = nkeys - n_main * tk
    body = functools.partial(_mla_attn_body, scale=MLA_QK_DIM ** -0.5, tk=tk, n_main=n_main, tail=tail)
    kb = k0 // nkeys
    once = pl.Buffered(1)
    return pl.pallas_call(
        body, grid=(hh, nq // tq),
        in_specs=[pl.BlockSpec((tq, MLA_NOPE_DIM), lambda h, i: (q0 // tq + i, h)),
                  pl.BlockSpec((tq, LANES), lambda h, i: (q0 // tq + i, h // 2)),
                  pl.BlockSpec((nkeys, MLA_NOPE_DIM), lambda h, i: (kb, 2 * h), pipeline_mode=once),
                  pl.BlockSpec((nkeys, LANES), lambda h, i: (kb, 0), pipeline_mode=once),
                  pl.BlockSpec((nkeys, MLA_V_DIM), lambda h, i: (kb, 2 * h + 1), pipeline_mode=once)],
        out_specs=pl.BlockSpec((tq, MLA_V_DIM), lambda h, i: (i, h)),
        out_shape=jax.ShapeDtypeStruct((nq, hh * MLA_V_DIM), BF16),
        scratch_shapes=[pltpu.VMEM((nkeys, MLA_QK_DIM), BF16), pltpu.VMEM((MLA_V_DIM, nkeys), BF16)],
        name=name, compiler_params=_params(("parallel", "arbitrary")),
    )(qn, qr, kv, kpe, kv)


def _mla_mixer(h, p, m, rope_c, rope_s, s_lat, with_ctx):
    t, d = h.shape
    n_ctx = t - s_lat
    hh = MLA_HEADS
    tm = _row_tile(t, MM_ROWS)
    cq = _matmul(h, p["mla_w_dq"][m].astype(BF16), out_dtype=BF16, tn=p["mla_w_dq"].shape[2],
                 epi=_epi_rmsnorm, epi_ops=[_row_vec_op(p["mla_q_norm"][m], p["mla_w_dq"].shape[2])], name="mla_dq")
    w_dkv = p["mla_w_dkv"][m]
    ckv = _matmul(h, w_dkv[:, :MLA_KV_RANK].astype(BF16), out_dtype=BF16, tn=MLA_KV_RANK,
                  epi=_epi_rmsnorm, epi_ops=[_row_vec_op(p["mla_kv_norm"][m], MLA_KV_RANK)], name="mla_dkv")
    w_pe = w_dkv[:, MLA_KV_RANK:].astype(BF16)
    kpe = _matmul(h, jnp.concatenate([w_pe, w_pe], axis=1), out_dtype=BF16, tn=LANES,
                  epi=_epi_rope, epi_ops=_rope_ops(rope_c, rope_s, tm), name="mla_kpe")
    w_uq = p["mla_w_uq"][m].astype(BF16).reshape(-1, hh, MLA_QK_DIM)
    qn = _matmul(cq, w_uq[:, :, :MLA_NOPE_DIM].reshape(-1, hh * MLA_NOPE_DIM), out_dtype=BF16, name="mla_uq_nope")
    qr = _matmul(cq, w_uq[:, :, MLA_NOPE_DIM:].reshape(-1, hh * MLA_ROPE_DIM), out_dtype=BF16,
                 epi=_epi_rope, epi_ops=_rope_ops(rope_c, rope_s, tm), name="mla_uq_rope")
    kv = _matmul(ckv, p["mla_w_ukv"][m].astype(BF16), out_dtype=BF16, name="mla_ukv")
    tq = 1024 if s_lat % 1024 == 0 else n_ctx
    o = _mla_attention(qn, qr, kv, kpe, q_rows=(0, s_lat), k_rows=(0, t), tq=tq, name="mla_attn_lat")
    if with_ctx:
        o_c = _mla_attention(qn, qr, kv, kpe, q_rows=(s_lat, n_ctx), k_rows=(s_lat, n_ctx), tq=n_ctx,
                             name="mla_attn_ctx")
        o = jnp.concatenate([o, o_c], axis=0)
    else:
        o = jnp.concatenate([o, jnp.zeros((n_ctx, o.shape[1]), o.dtype)], axis=0)
    return _matmul(o, p["mla_w_o"][m].astype(BF16), out_dtype=BF16, name="mla_o")


def _swa_body(sink_ref, q_ref, kp_ref, kc_ref, kn_ref, kx_ref, vp_ref, vc_ref, vn_ref, vx_ref, o_ref,
              *, scale, nb, grp, latent):
    j = pl.program_id(0)
    kh = pl.program_id(1)
    qb = q_ref[...]
    blk = qb.shape[0]
    dh = SWA_HEAD_DIM
    if latent:
        keys = jnp.concatenate([kx_ref[0], kp_ref[0], kc_ref[0], kn_ref[0]], axis=0)
        vals = jnp.concatenate([vx_ref[0], vp_ref[0], vc_ref[0], vn_ref[0]], axis=0)
        n_ctx = kx_ref.shape[1]
        nkeys = keys.shape[0]
        r = lax.broadcasted_iota(jnp.int32, (blk, nkeys), 0)
        c = lax.broadcasted_iota(jnp.int32, (blk, nkeys), 1) - n_ctx
        rel = c - blk - r
        kpos = (j - 1) * blk + c
        ok = (c < 0) | ((jnp.abs(rel) <= SWA_WINDOW) & (kpos >= 0) & (kpos < nb * blk))
    else:
        keys, vals = kx_ref[0], vx_ref[0]
        ok = None
    outs = []
    for g in range(grp):
        qg = (qb[:, g * dh:(g + 1) * dh].astype(F32) * scale).astype(BF16)
        s = lax.dot_general(qg, keys, (((1,), (1,)), ((), ())), preferred_element_type=F32)
        if ok is not None:
            s = jnp.where(ok, s, -jnp.inf)
        sink = sink_ref[kh * grp + g]
        m = jnp.maximum(jnp.max(s, axis=1, keepdims=True), sink)
        pr = jnp.exp(s - m)
        denom = jnp.sum(pr, axis=1, keepdims=True) + jnp.exp(sink - m)
        og = jnp.dot(pr.astype(BF16), vals, preferred_element_type=F32) / denom
        outs.append(og)
    o_ref[...] = jnp.concatenate(outs, axis=1).astype(o_ref.dtype)


def _swa_attention(q, k, v, sink, s_lat, latent):
    t = q.shape[0]
    n_ctx = t - s_lat
    grp = SWA_HEADS // SWA_KV_HEADS
    blk = SWA_BLOCK
    dh = SWA_HEAD_DIM
    width = grp * dh
    if latent:
        nb = s_lat // blk
        q0 = 0
    else:
        nb = 1
        blk = n_ctx
        q0 = s_lat // n_ctx
    ctx_blk = s_lat // n_ctx

    def kv_spec(shift):
        return pl.BlockSpec((1, SWA_BLOCK, dh), lambda j, kh: (kh, jnp.clip(j + shift, 0, max(nb - 1, 0)), 0))

    ctx_spec = pl.BlockSpec((1, n_ctx, dh), lambda j, kh: (kh, ctx_blk, 0))
    body = functools.partial(_swa_body, scale=dh ** -0.5, nb=nb, grp=grp, latent=latent)
    win = [kv_spec(-1), kv_spec(0), kv_spec(1)]
    return pl.pallas_call(
        body, grid=(nb, SWA_KV_HEADS),
        in_specs=[pl.BlockSpec(memory_space=pltpu.SMEM),
                  pl.BlockSpec((blk, width), lambda j, kh: (q0 + j, kh))] + win + [ctx_spec] + win + [ctx_spec],
        out_specs=pl.BlockSpec((blk, width), lambda j, kh: (j, kh)),
        out_shape=jax.ShapeDtypeStruct((nb * blk, SWA_HEADS * dh), BF16),
        name="swa_attn_lat" if latent else "swa_attn_ctx",
        compiler_params=_params(("parallel", "parallel")),
    )(sink.astype(F32), q, k, k, k, k, v, v, v, v)


def _swa_mixer(h, p, m, rope_c, rope_s, s_lat, with_ctx):
    t, d = h.shape
    n_ctx = t - s_lat
    tm = _row_tile(t, MM_ROWS)
    qw = SWA_HEADS * SWA_HEAD_DIM
    kw = SWA_KV_HEADS * SWA_HEAD_DIM
    w = p["swa_w_qkv"][m].astype(BF16)
    b = p["swa_b_qkv"][m]
    tn = 512
    qk = _matmul(h, w[:, :qw + kw], out_dtype=BF16, tn=tn, epi=_epi_bias_rope,
                 epi_ops=[_row_vec_op(b[:qw + kw], tn)] + _rope_ops(rope_c, rope_s, tm), name="swa_qk")
    vv = _matmul(h, w[:, qw + kw:], out_dtype=BF16, tn=tn, epi=_epi_bias,
                 epi_ops=[_row_vec_op(b[qw + kw:], tn)], name="swa_v")
    q = qk[:, :qw]
    k = qk[:, qw:].reshape(t, SWA_KV_HEADS, SWA_HEAD_DIM).transpose(1, 0, 2)
    v = vv.reshape(t, SWA_KV_HEADS, SWA_HEAD_DIM).transpose(1, 0, 2)
    sink = p["swa_sink"][m]
    o = _swa_attention(q, k, v, sink, s_lat, True)
    if with_ctx:
        o = jnp.concatenate([o, _swa_attention(q, k, v, sink, s_lat, False)], axis=0)
    else:
        o = jnp.concatenate([o, jnp.zeros((n_ctx, o.shape[1]), o.dtype)], axis=0)
    return _matmul(o, p["swa_w_o"][m].astype(BF16), out_dtype=BF16, epi=_epi_bias,
                   epi_ops=[_row_vec_op(p["swa_b_o"][m], 1024)], tn=1024, name="swa_o")


def _perm_cols(w):
    lead = w.shape[:-1]
    nh = w.shape[-1] // RWKV_HEAD
    return jnp.swapaxes(w.reshape(lead + (nh, RWKV_HEAD)), -1, -2).reshape(lead + (nh * RWKV_HEAD,))


def _rwkv_mix_body(h_ref, hp_ref, hn_ref, mu_ref, *out_refs, nb_lat, nb):
    i = pl.program_id(0)
    h = h_ref[...].astype(F32)
    tm = h.shape[0]
    first = (i == 0) | (i == nb_lat)
    last = (i == nb_lat - 1) | (i == nb - 1)
    prev_row = jnp.where(first, 0.0, hp_ref[SUBLANES - 1:SUBLANES, :].astype(F32))
    next_row = jnp.where(last, 0.0, hn_ref[0:1, :].astype(F32))
    rid = lax.broadcasted_iota(jnp.int32, (tm, 1), 0)
    prev = jnp.where(rid == 0, prev_row, pltpu.roll(h, 1, 0))
    nxt = jnp.where(rid == tm - 1, next_row, pltpu.roll(h, tm - 1, 0))
    xx = 0.5 * (prev + nxt) - h
    for n, o_ref in enumerate(out_refs):
        o_ref[...] = (h + xx * mu_ref[n:n + 1, :]).astype(o_ref.dtype)


def _rwkv_mix(h, mu, s_lat):
    t, d = h.shape
    tm = 128
    nb, nb_lat = t // tm, s_lat // tm
    r8 = tm // SUBLANES
    body = functools.partial(_rwkv_mix_body, nb_lat=nb_lat, nb=nb)
    row = pl.BlockSpec((tm, d), lambda i: (i, 0))
    return pl.pallas_call(
        body, grid=(nb,),
        in_specs=[row,
                  pl.BlockSpec((SUBLANES, d), lambda i: (jnp.maximum(i * r8 - 1, 0), 0)),
                  pl.BlockSpec((SUBLANES, d), lambda i: (jnp.minimum((i + 1) * r8, t // SUBLANES - 1), 0)),
                  pl.BlockSpec((6, d), lambda i: (0, 0))],
        out_specs=[row] * 6, out_shape=[jax.ShapeDtypeStruct((t, d), BF16)] * 6, name="rwkv_mix",
        compiler_params=_params(("parallel",)),
    )(h, h, h, mu)


def _head_sum(ref_or_val_slabs):
    acc = ref_or_val_slabs[0]
    for s in ref_or_val_slabs[1:]:
        acc = acc + s
    return acc + pltpu.roll(acc, 64, 1)


def _rwkv_prep_body(r_ref, k_ref, v_ref, wl0_ref, wl1_ref, al0_ref, al1_ref, kk_ref, ka_ref, rk_ref,
                    d0_ref, d1_ref, kz0_ref, kz1_ref, b0_ref, b1_ref, na_ref, bonus_ref, vdup_ref):
    nsl = r_ref.shape[1] // LANES
    sl = lambda ref, j: ref[:, j * LANES:(j + 1) * LANES]
    kks = [sl(k_ref, j) * sl(kk_ref, j) for j in range(nsl)]
    ss = _head_sum([x * x for x in kks])
    inv = lax.rsqrt(jnp.maximum(ss, 1e-24))
    bon = [None, None]
    kz_all = [[], []]
    for z, (wl_ref, al_ref, d_ref, kz_ref, b_ref) in enumerate(
            ((wl0_ref, al0_ref, d0_ref, kz0_ref, b0_ref), (wl1_ref, al1_ref, d1_ref, kz1_ref, b1_ref))):
        prods = []
        for j in range(nsl):
            kkn = kks[j] * inv
            if z == 0:
                na_ref[:, j * LANES:(j + 1) * LANES] = -kkn
            logw = -jax.nn.softplus(-sl(wl_ref, j)) - 0.5
            d_ref[:, j * LANES:(j + 1) * LANES] = jnp.exp(-jnp.exp(logw))
            a = jax.nn.sigmoid(sl(al_ref, j))
            kz = sl(k_ref, j) * (1.0 + (a - 1.0) * sl(ka_ref, j))
            kz_ref[:, j * LANES:(j + 1) * LANES] = kz
            b_ref[:, j * LANES:(j + 1) * LANES] = kkn * a
            prods.append(sl(r_ref, j) * kz * sl(rk_ref, j))
        bon[z] = _head_sum(prods)
    tot = bon[0] + bon[1]
    lane = lax.broadcasted_iota(jnp.int32, (1, LANES), 1)
    for j in range(nsl):
        vs = sl(v_ref, j)
        bonus_ref[:, j * LANES:(j + 1) * LANES] = tot * vs
        sw = pltpu.roll(vs, 64, 1)
        vdup_ref[:, (2 * j) * LANES:(2 * j + 1) * LANES] = jnp.where(lane < 64, vs, sw)
        vdup_ref[:, (2 * j + 1) * LANES:(2 * j + 2) * LANES] = jnp.where(lane < 64, sw, vs)


def _rwkv_prep(r, k, v, wl, al, k_k, k_a, r_k):
    t, d = r.shape
    tm = 64
    row = pl.BlockSpec((tm, d), lambda i: (i, 0))
    row2 = pl.BlockSpec((tm, 2 * d), lambda i: (i, 0))
    vec = pl.BlockSpec((1, d), lambda i: (0, 0))
    return pl.pallas_call(
        _rwkv_prep_body, grid=(t // tm,),
        in_specs=[row] * 7 + [vec] * 3,
        out_specs=[row] * 8 + [row2],
        out_shape=[jax.ShapeDtypeStruct((t, d), F32)] * 8 + [jax.ShapeDtypeStruct((t, 2 * d), F32)],
        name="rwkv_prep", compiler_params=_params(("parallel",)),
    )(r, k, v, wl[0], wl[1], al[0], al[1], k_k.reshape(1, d), k_a.reshape(1, d), r_k.reshape(1, d))


def _wkv_body(r_ref, w_ref, k_ref, a_ref, b_ref, v_ref, y_ref, s_ref, *, tb, reverse):
    @pl.when(pl.program_id(0) == 0)
    def _():
        s_ref[...] = jnp.zeros_like(s_ref)

    nv = s_ref.shape[0]

    def fold(x):
        rows = jnp.sum(x, axis=0, keepdims=True)
        return rows + pltpu.roll(rows, 64, 1)

    def step(i, carry):
        t = (tb - 1 - i) if reverse else i
        a2, w2, b2, k2, r2 = a_ref[t], w_ref[t], b_ref[t], k_ref[t], r_ref[t]
        for v in range(nv):
            st = s_ref[v]
            vrow = v_ref[t, pl.ds(v, 1), :]
            sa = fold(st * a2)
            sn = st * w2 + sa * b2 + vrow * k2
            s_ref[v] = sn
            y_ref[t, pl.ds(v, 1), :] = fold(sn * r2)
        return carry

    lax.fori_loop(0, tb, step, 0)


def _wkv_scan(r, w, k, a, b, vdup, s_lat, reverse):
    t, d = r.shape
    n = RWKV_HEAD
    nh = d // n
    tb = 64
    nb, nb_lat = t // tb, s_lat // tb
    nb_ctx = nb - nb_lat
    if reverse:
        order = lambda i: nb - 1 - i
    else:
        order = lambda i: jnp.where(i < nb_ctx, i + nb_lat, i - nb_ctx)
    kspec = pl.BlockSpec((tb, n // 2, 2 * nh), lambda i: (order(i), 0, 0))
    vspec = pl.BlockSpec((tb, n, 2 * nh), lambda i: (order(i), 0, 0))
    as3 = lambda z: z.reshape(t, n // 2, 2 * nh)
    body = functools.partial(_wkv_body, tb=tb, reverse=reverse)
    y = pl.pallas_call(
        body, grid=(nb,),
        in_specs=[kspec] * 5 + [vspec], out_specs=vspec,
        out_shape=jax.ShapeDtypeStruct((t, n, 2 * nh), F32),
        scratch_shapes=[pltpu.VMEM((n, n // 2, 2 * nh), F32)],
        name="wkv_rev" if reverse else "wkv_fwd",
        compiler_params=_params(("arbitrary",)),
    )(as3(r), as3(w), as3(k), as3(a), as3(b), vdup.reshape(t, n, 2 * nh))
    return y.reshape(t, 2 * d)


def _rwkv_finish_body(y0_ref, y1_ref, bonus_ref, g_ref, lw_ref, lb_ref, o_ref):
    nsl = bonus_ref.shape[1] // LANES
    lane = lax.broadcasted_iota(jnp.int32, (1, LANES), 1)
    ysl = lambda ref, j: ref[:, j * LANES:(j + 1) * LANES]
    ys = [jnp.where(lane < 64, ysl(y0_ref, 2 * j) + ysl(y1_ref, 2 * j), ysl(y0_ref, 2 * j + 1) + ysl(y1_ref, 2 * j + 1))
          for j in range(nsl)]
    mean = _head_sum(ys) * (1.0 / RWKV_HEAD)
    cs = [y - mean for y in ys]
    var = _head_sum([c * c for c in cs]) * (1.0 / RWKV_HEAD)
    inv = lax.rsqrt(var + RWKV_GN_EPS)
    for j in range(nsl):
        s = slice(j * LANES, (j + 1) * LANES)
        out = cs[j] * inv * lw_ref[:, s] + lb_ref[:, s] + bonus_ref[:, s]
        o_ref[:, s] = (out * g_ref[:, s].astype(F32)).astype(o_ref.dtype)


def _rwkv_finish(y0, y1, bonus, g, ln_w, ln_b):
    t, d = bonus.shape
    tm = 128
    row = pl.BlockSpec((tm, d), lambda i: (i, 0))
    row2 = pl.BlockSpec((tm, 2 * d), lambda i: (i, 0))
    vec = pl.BlockSpec((1, d), lambda i: (0, 0))
    return pl.pallas_call(
        _rwkv_finish_body, grid=(t // tm,), in_specs=[row2] * 2 + [row] * 2 + [vec] * 2, out_specs=row,
        out_shape=jax.ShapeDtypeStruct((t, d), BF16), name="rwkv_finish",
        compiler_params=_params(("parallel",)),
    )(y0, y1, bonus, g, ln_w.reshape(1, d), ln_b.reshape(1, d))


def _rwkv_mixer(h, p, m, s_lat):
    t, d = h.shape
    mixes = _rwkv_mix(h, p["rwkv_mu"][m], s_lat)
    w_rkv = _perm_cols(p["rwkv_w_rkv"][m]).astype(BF16)
    r = _matmul(mixes[0], w_rkv[0], out_dtype=F32, name="rwkv_r")
    k = _matmul(mixes[1], w_rkv[1], out_dtype=F32, name="rwkv_k")
    v = _matmul(mixes[2], w_rkv[2], out_dtype=F32, name="rwkv_v")
    lora = p["rwkv_w1"].shape[-1]
    w1 = jnp.concatenate([p["rwkv_w1"][m, 0], p["rwkv_w1"][m, 1]], axis=1).astype(BF16)
    a1 = jnp.concatenate([p["rwkv_a1"][m, 0], p["rwkv_a1"][m, 1]], axis=1).astype(BF16)
    wmid = _matmul(mixes[3], w1, out_dtype=BF16, epi=functools.partial(_epi_act, act=jnp.tanh), name="rwkv_w1")
    amid = _matmul(mixes[4], a1, out_dtype=BF16, name="rwkv_a1")
    w2 = _perm_cols(p["rwkv_w2"][m]).astype(BF16)
    a2 = _perm_cols(p["rwkv_a2"][m]).astype(BF16)
    w0 = _perm_cols(p["rwkv_w0"][m])
    a0 = _perm_cols(p["rwkv_a0"][m])
    wl, al = [], []
    for z in range(2):
        wl.append(_matmul(wmid[:, z * lora:(z + 1) * lora], w2[z], out_dtype=F32, epi=_epi_bias,
                          epi_ops=[_row_vec_op(w0[z], 1024)], tn=1024, name="rwkv_w2"))
        al.append(_matmul(amid[:, z * lora:(z + 1) * lora], a2[z], out_dtype=F32, epi=_epi_bias,
                          epi_ops=[_row_vec_op(a0[z], 1024)], tn=1024, name="rwkv_a2"))
    gmid = _matmul(mixes[5], p["rwkv_g1"][m].astype(BF16), out_dtype=BF16,
                   epi=functools.partial(_epi_act, act=jax.nn.sigmoid), name="rwkv_g1")
    g = _matmul(gmid, _perm_cols(p["rwkv_g2"][m]).astype(BF16), out_dtype=BF16, name="rwkv_g2")
    d0, d1, kz0, kz1, b0, b1, na, bonus, vdup = _rwkv_prep(
        r, k, v, wl, al, _perm_cols(p["rwkv_k_k"][m]), _perm_cols(p["rwkv_k_a"][m]), _perm_cols(p["rwkv_r_k"][m]))
    y0 = _wkv_scan(r, d0, kz0, na, b0, vdup, s_lat, False)
    y1 = _wkv_scan(r, d1, kz1, na, b1, vdup, s_lat, True)
    out = _rwkv_finish(y0, y1, bonus, g, _perm_cols(p["rwkv_ln_w"][m]), _perm_cols(p["rwkv_ln_b"][m]))
    w_o = p["rwkv_w_o"][m]
    nh = d // RWKV_HEAD
    w_o = w_o.reshape(nh, RWKV_HEAD, d).transpose(1, 0, 2).reshape(d, d).astype(BF16)
    return _matmul(out, w_o, out_dtype=BF16, name="rwkv_o")


def _epi_moe_act(acc, ids, b1_ref, gates_ref):
    e = ids[1]
    u = acc + b1_ref[0]
    x_glu = jnp.minimum(u[:, :EXPERT_FF], SWIGLU_LIMIT)
    x_lin = jnp.clip(u[:, EXPERT_FF:], -SWIGLU_LIMIT, SWIGLU_LIMIT)
    act = x_glu * jax.nn.sigmoid(SWIGLU_ALPHA * x_glu) * (x_lin + 1.0)
    gates = gates_ref[...]
    lane = lax.broadcasted_iota(jnp.int32, gates.shape, 1)
    gcol = jnp.sum(jnp.where(lane == e, gates, 0.0), axis=1, keepdims=True)
    return act * gcol


def _epi_moe_out(acc, ids, gates_ref, b2_ref):
    return acc + jnp.dot(gates_ref[...], b2_ref[...], preferred_element_type=F32)


def _moe(h2, gates, w1, b1, w2, b2):
    t, d = h2.shape
    e, _, f2 = w1.shape
    ff = f2 // 2
    tm = _row_tile(t, MM_ROWS)
    gate_spec = pl.BlockSpec((tm, e), lambda i, j, k: (i, 0))
    act = _matmul(
        h2, w1.astype(BF16), out_dtype=BF16, tm=tm, tn=f2, tk=d, n_total=e * f2,
        w_spec=pl.BlockSpec((None, d, f2), lambda i, j, k: (j, 0, 0)),
        epi=_epi_moe_act,
        epi_ops=[(b1.reshape(e, 1, f2), pl.BlockSpec((1, 1, f2), lambda i, j, k: (j, 0, 0))), (gates, gate_spec)],
        out_shape=jax.ShapeDtypeStruct((t, e * ff), BF16),
        out_spec=pl.BlockSpec((tm, ff), lambda i, j, k: (i, j)), name="moe_up")
    tn = 1024
    return _matmul(
        act, w2.astype(BF16).reshape(e * ff, d), out_dtype=BF16, tm=tm, tn=tn, epi=_epi_moe_out,
        epi_ops=[(gates, gate_spec), (b2, pl.BlockSpec((e, tn), lambda i, j, k: (0, j)))], name="moe_down")


def kernel(x, c, ctx, c_ctx, ada_a, ada_b, ada_bias, norm_g, mla_w_dq, mla_q_norm, mla_w_uq, mla_w_dkv, mla_kv_norm, mla_w_ukv, mla_w_o, rwkv_mu, rwkv_w_rkv, rwkv_w0, rwkv_w1, rwkv_w2, rwkv_a0, rwkv_a1, rwkv_a2, rwkv_g1, rwkv_g2, rwkv_k_k, rwkv_k_a, rwkv_r_k, rwkv_ln_w, rwkv_ln_b, rwkv_w_o, swa_w_qkv, swa_b_qkv, swa_sink, swa_w_o, swa_b_o, moe_w_router, moe_b_router, moe_w1, moe_b1, moe_w2, moe_b2):
    p = dict(mla_w_dq=mla_w_dq, mla_q_norm=mla_q_norm, mla_w_uq=mla_w_uq, mla_w_dkv=mla_w_dkv,
             mla_kv_norm=mla_kv_norm, mla_w_ukv=mla_w_ukv, mla_w_o=mla_w_o,
             rwkv_mu=rwkv_mu, rwkv_w_rkv=rwkv_w_rkv, rwkv_w0=rwkv_w0, rwkv_w1=rwkv_w1, rwkv_w2=rwkv_w2,
             rwkv_a0=rwkv_a0, rwkv_a1=rwkv_a1, rwkv_a2=rwkv_a2, rwkv_g1=rwkv_g1, rwkv_g2=rwkv_g2,
             rwkv_k_k=rwkv_k_k, rwkv_k_a=rwkv_k_a, rwkv_r_k=rwkv_r_k, rwkv_ln_w=rwkv_ln_w,
             rwkv_ln_b=rwkv_ln_b, rwkv_w_o=rwkv_w_o, swa_w_qkv=swa_w_qkv, swa_b_qkv=swa_b_qkv,
             swa_sink=swa_sink, swa_w_o=swa_w_o, swa_b_o=swa_b_o)
    batch, s_lat, d = x.shape
    n_ctx = ctx.shape[1]
    assert batch == 1 and s_lat % n_ctx == 0 and n_ctx % EW_ROWS == 0
    depth = ada_a.shape[0]
    xs = jnp.concatenate([x[0], ctx[0]], axis=0)
    cond8 = jnp.concatenate([c, c_ctx[None, :], jnp.zeros((SUBLANES - 2, d), F32)], axis=0)
    rope_c, rope_s = _rope_tables(s_lat, n_ctx)
    mods = [_adaln(cond8, ada_a[i], ada_b[i], ada_bias[i]) for i in range(depth)]
    h = _norm_mod(xs, norm_g[0, 0], mods[0], s_lat)
    for i in range(depth):
        last = i == depth - 1
        kind, m = i % N_MIXERS, i // N_MIXERS
        if kind == 0:
            y = _mla_mixer(h, p, m, rope_c, rope_s, s_lat, not last)
        elif kind == 1:
            y = _rwkv_mixer(h, p, m, s_lat)
        else:
            y = _swa_mixer(h, p, m, rope_c, rope_s, s_lat, not last)
        xs, h2, gates = _resid_norm(xs, y, norm_g[i, 1], mods[i], 2, s_lat, gb=norm_g[i, 2], mods_b=mods[i],
                                    mod_rows=(3, 4), router=(moe_w_router[i], moe_b_router[i]))
        f = _moe(h2, gates, moe_w1[i], moe_b1[i], moe_w2[i], moe_b2[i])
        if last:
            (xs,) = _resid_norm(xs, f, norm_g[i, 3], mods[i], 5, s_lat)
        else:
            xs, h = _resid_norm(xs, f, norm_g[i, 3], mods[i], 5, s_lat, gb=norm_g[i + 1, 0],
                                mods_b=mods[i + 1], mod_rows=(0, 1))
    return xs[:s_lat][None]
```

```python
import functools

import jax
import jax.numpy as jnp
from jax import lax
from jax.experimental import pallas as pl
from jax.experimental.pallas import tpu as pltpu

F32 = jnp.float32
BF16 = jnp.bfloat16

GRID_W = 64
ROPE_BASE = 10000.0
NORM_EPS = 1e-6
MLA_HEADS = 64
MLA_KV_RANK = 512
MLA_NOPE_DIM = 128
MLA_ROPE_DIM = 64
MLA_V_DIM = 128
MLA_QK_DIM = MLA_NOPE_DIM + MLA_ROPE_DIM
RWKV_HEAD = 64
RWKV_GN_EPS = 64e-5
SWA_HEADS = 64
SWA_KV_HEADS = 8
SWA_HEAD_DIM = 64
SWA_WINDOW = 128
SWA_BLOCK = 128
N_EXPERTS = 32
TOP_K = 4
EXPERT_FF = 384
SWIGLU_ALPHA = 1.702
SWIGLU_LIMIT = 7.0
N_MIXERS = 3

LANES = 128
SUBLANES = 8
VMEM_LIMIT = 56 * 1024 * 1024
LOG2E = 1.4426950408889634

MM_ROWS = 1280
MM_K = 4096
EW_ROWS = 256
ATTN_Q_ROWS = 2048
ATTN_K_ROWS = 1024
ATTN_MAX_RISE = 64.0


def _params(sem):
    return pltpu.CompilerParams(dimension_semantics=sem, vmem_limit_bytes=VMEM_LIMIT)


def _row_tile(t, pref):
    if t <= pref:
        return t
    best = None
    for cand in range(pref, 7, -8):
        if t % cand == 0:
            best = cand
            break
    assert best is not None, (t, pref)
    return best


def _col_tile(n, pref):
    if n <= pref:
        return n
    for cand in range(pref, LANES - 1, -LANES):
        if n % cand == 0:
            return cand
    return n


def _mm_body(*refs, nk, n_epi, epi, store):
    x_ref, w_ref = refs[0], refs[1]
    epi_refs = refs[2:2 + n_epi]
    o_ref = refs[2 + n_epi]
    ids = (pl.program_id(0), pl.program_id(1))
    part = jnp.dot(x_ref[...].astype(BF16), w_ref[...], preferred_element_type=F32)
    if nk == 1:
        store(o_ref, epi(part, ids, *epi_refs))
        return
    acc_ref = refs[3 + n_epi]
    k = pl.program_id(2)

    @pl.when(k == 0)
    def _():
        acc_ref[...] = part

    @pl.when(k > 0)
    def _():
        acc_ref[...] += part

    @pl.when(k == nk - 1)
    def _():
        store(o_ref, epi(acc_ref[...], ids, *epi_refs))


def _store_plain(o_ref, val):
    o_ref[...] = val.astype(o_ref.dtype)


def _epi_none(acc, ids):
    return acc


def _matmul(x, w, *, out_dtype, tm=None, tn=None, tk=None, epi=_epi_none, epi_ops=(),
            w_spec=None, n_total=None, out_shape=None, out_spec=None, store=_store_plain, name="mm"):
    m, kdim = x.shape
    n = n_total if n_total is not None else w.shape[1]
    tm = tm or _row_tile(m, MM_ROWS)
    tk = tk or _col_tile(kdim, MM_K)
    tn = tn or _col_tile(n, 1024 if tk <= 1024 else 512)
    assert m % tm == 0 and n % tn == 0 and kdim % tk == 0, (m, n, kdim, tm, tn, tk)
    nk = kdim // tk
    grid = (m // tm, n // tn, nk)
    if w_spec is None:
        w_spec = pl.BlockSpec((tk, tn), lambda i, j, k: (k, j))
    in_specs = [pl.BlockSpec((tm, tk), lambda i, j, k: (i, k)), w_spec] + [s for _, s in epi_ops]
    if out_shape is None:
        out_shape = jax.ShapeDtypeStruct((m, n), out_dtype)
        out_spec = pl.BlockSpec((tm, tn), lambda i, j, k: (i, j))
    scratch = [pltpu.VMEM((tm, tn), F32)] if nk > 1 else []
    body = functools.partial(_mm_body, nk=nk, n_epi=len(epi_ops), epi=epi, store=store)
    return pl.pallas_call(
        body, grid=grid, in_specs=in_specs, out_specs=out_spec, out_shape=out_shape,
        scratch_shapes=scratch, name=name,
        compiler_params=_params(("parallel", "parallel", "arbitrary")),
    )(x, w, *[a for a, _ in epi_ops])


def _row_vec_op(vec, tn):
    return (vec.reshape(1, -1).astype(F32), pl.BlockSpec((1, tn), lambda i, j, k: (0, j)))


def _epi_bias(acc, ids, b_ref):
    return acc + b_ref[...]


def _epi_act(acc, ids, *, act):
    return act(acc)


def _epi_rmsnorm(acc, ids, g_ref):
    ms = jnp.mean(acc * acc, axis=-1, keepdims=True)
    return acc * lax.rsqrt(ms + NORM_EPS) * g_ref[...]


def _rope_cols(x, c, sg):
    lane = lax.broadcasted_iota(jnp.int32, (1, LANES), 1)
    first_half = (lane % 64) < 32
    outs = []
    for g in range(x.shape[1] // LANES):
        xg = x[:, g * LANES:(g + 1) * LANES]
        partner = jnp.where(first_half, pltpu.roll(xg, LANES - 32, 1), pltpu.roll(xg, 32, 1))
        outs.append(xg * c + partner * sg)
    return outs[0] if len(outs) == 1 else jnp.concatenate(outs, axis=1)


def _epi_rope(acc, ids, c_ref, s_ref):
    return _rope_cols(acc, c_ref[...], s_ref[...])


def _epi_bias_rope(acc, ids, b_ref, c_ref, s_ref):
    return _rope_cols(acc + b_ref[...], c_ref[...], s_ref[...])


def _rope_ops(rope_c, rope_s, tm):
    spec = pl.BlockSpec((tm, LANES), lambda i, j, k: (i, 0))
    return [(rope_c, spec), (rope_s, spec)]


def _rope_tables(s_lat, n_ctx):
    t = jnp.arange(s_lat)
    row = (t // GRID_W).astype(F32)
    col = (t % GRID_W).astype(F32)
    axis_dim = 32
    inv = ROPE_BASE ** (-jnp.arange(0, axis_dim, 2, dtype=F32) / axis_dim)
    ang = jnp.concatenate([row[:, None] * inv, col[:, None] * inv], axis=-1)
    cos, sin = jnp.cos(ang), jnp.sin(ang)
    c64 = jnp.concatenate([cos, cos], axis=-1)
    s64 = jnp.concatenate([-sin, sin], axis=-1)
    c = jnp.concatenate([c64, c64], axis=-1)
    s = jnp.concatenate([s64, s64], axis=-1)
    c = jnp.concatenate([c, jnp.ones((n_ctx, LANES), F32)], axis=0)
    s = jnp.concatenate([s, jnp.zeros((n_ctx, LANES), F32)], axis=0)
    return c, s


def _rms(x, g):
    return x * lax.rsqrt(jnp.mean(x * x, axis=-1, keepdims=True) + NORM_EPS) * g


def _norm_mod_body(x_ref, g_ref, mod_ref, h_ref):
    h = _rms(x_ref[...], g_ref[...]) * (1.0 + mod_ref[0, 1:2, :]) + mod_ref[0, 0:1, :]
    h_ref[...] = h.astype(h_ref.dtype)


def _norm_mod(x, g, mods, s_lat):
    t, d = x.shape
    tm = EW_ROWS
    nb_lat = s_lat // tm
    return pl.pallas_call(
        _norm_mod_body, grid=(t // tm,),
        in_specs=[pl.BlockSpec((tm, d), lambda i: (i, 0)),
                  pl.BlockSpec((1, d), lambda i: (0, 0)),
                  pl.BlockSpec((1, 6, d), lambda i: (jnp.where(i >= nb_lat, 1, 0), 0, 0))],
        out_specs=pl.BlockSpec((tm, d), lambda i: (i, 0)),
        out_shape=jax.ShapeDtypeStruct((t, d), BF16), name="norm_mod",
        compiler_params=_params(("parallel",)),
    )(x, g.reshape(1, d), mods)


def _top4_gates(logits):
    lane = lax.broadcasted_iota(jnp.int32, logits.shape, 1)
    work = logits
    gate = jnp.zeros_like(logits)
    denom = jnp.zeros((logits.shape[0], 1), F32)
    m0 = None
    for _ in range(TOP_K):
        m = jnp.max(work, axis=1, keepdims=True)
        idx = jnp.min(jnp.where(work == m, lane, N_EXPERTS), axis=1, keepdims=True)
        hit = lane == idx
        if m0 is None:
            m0 = m
        wk = jnp.exp(m - m0)
        gate = gate + jnp.where(hit, wk, 0.0)
        denom = denom + wk
        work = jnp.where(hit, -jnp.inf, work)
    return gate / denom


def _resid_body(*refs, gate_row, mod_rows, router):
    x_ref, y_ref, ga_ref, moda_ref = refs[:4]
    pos = 4
    xn = x_ref[...] + moda_ref[0, gate_row:gate_row + 1, :] * _rms(y_ref[...].astype(F32), ga_ref[...])
    if mod_rows is None:
        refs[pos][...] = xn
        return
    gb_ref, modb_ref = refs[pos], refs[pos + 1]
    pos += 2
    if router:
        wr_ref, br_ref = refs[pos], refs[pos + 1]
        pos += 2
    xn_ref, h_ref = refs[pos], refs[pos + 1]
    xn_ref[...] = xn
    sh, sc = mod_rows
    h = _rms(xn, gb_ref[...]) * (1.0 + modb_ref[0, sc:sc + 1, :]) + modb_ref[0, sh:sh + 1, :]
    h_ref[...] = h.astype(h_ref.dtype)
    if router:
        gates_ref = refs[pos + 2]
        logits = jnp.dot(h, wr_ref[...], precision=lax.Precision.HIGHEST,
                         preferred_element_type=F32) + br_ref[...]
        gates_ref[...] = _top4_gates(logits)


def _resid_norm(x, y, ga, mods_a, gate_row, s_lat, gb=None, mods_b=None, mod_rows=None, router=None):
    t, d = x.shape
    tm = EW_ROWS
    nb_lat = s_lat // tm
    row = pl.BlockSpec((tm, d), lambda i: (i, 0))
    vec = pl.BlockSpec((1, d), lambda i: (0, 0))
    mod = pl.BlockSpec((1, 6, d), lambda i: (jnp.where(i >= nb_lat, 1, 0), 0, 0))
    ins = [x, y, ga.reshape(1, d), mods_a]
    in_specs = [row, row, vec, mod]
    out_shape = [jax.ShapeDtypeStruct((t, d), F32)]
    out_specs = [row]
    if mod_rows is not None:
        ins += [gb.reshape(1, d), mods_b]
        in_specs += [vec, mod]
        out_shape.append(jax.ShapeDtypeStruct((t, d), BF16))
        out_specs.append(row)
        if router is not None:
            wr, br = router
            ins += [wr, br.reshape(1, N_EXPERTS)]
            in_specs += [pl.BlockSpec((d, N_EXPERTS), lambda i: (0, 0)),
                         pl.BlockSpec((1, N_EXPERTS), lambda i: (0, 0))]
            out_shape.append(jax.ShapeDtypeStruct((t, N_EXPERTS), F32))
            out_specs.append(pl.BlockSpec((tm, N_EXPERTS), lambda i: (i, 0)))
    body = functools.partial(_resid_body, gate_row=gate_row, mod_rows=mod_rows, router=router is not None)
    outs = pl.pallas_call(
        body, grid=(t // tm,), in_specs=in_specs, out_specs=out_specs, out_shape=out_shape,
        name="resid_norm", compiler_params=_params(("parallel",)),
    )(*ins)
    return outs


def _silu(x):
    return x * jax.nn.sigmoid(x)


def _adaln(cond8, a, b, bias):
    d = cond8.shape[1]
    sil = _silu(cond8)
    low = _matmul(sil, a.astype(BF16), out_dtype=F32, tm=8, name="ada_a")
    m = _matmul(low, b.astype(BF16), out_dtype=F32, tm=8, epi=_epi_bias,
                epi_ops=[_row_vec_op(bias, 1024)], tn=1024, name="ada_b")
    return m[:2].reshape(2, 6, d)


def _mla_attn_body(qn_ref, qr_ref, kn_ref, kpe_ref, v_ref, o_ref, k_scr, vt_scr, m_scr, l_scr, acc_scr,
                   *, scale, tk, n_main, tail):
    hd = pl.program_id(0)
    nkeys = k_scr.shape[0]
    vchunk = 256

    @pl.when(pl.program_id(1) == 0)
    def _():
        k_scr[:, :MLA_NOPE_DIM] = kn_ref[...]
        k_scr[:, MLA_NOPE_DIM:] = kpe_ref[:, :MLA_ROPE_DIM]

        def tr(c, carry):
            off = pl.multiple_of(c * vchunk, vchunk)
            vt_scr[:, pl.ds(off, vchunk)] = v_ref[pl.ds(off, vchunk), :].astype(F32).T.astype(BF16)
            return carry

        lax.fori_loop(0, nkeys // vchunk, tr, 0)

    c0 = scale * LOG2E
    qn_t = qn_ref[...].astype(F32).T * c0
    qr_t = qr_ref[...].astype(F32).T * c0
    qr_t = jnp.where(hd % 2 == 0, qr_t[:MLA_ROPE_DIM], qr_t[MLA_ROPE_DIM:])
    qt = jnp.concatenate([qn_t, qr_t], axis=0).astype(BF16)

    def two_pass(ks, vts):
        m = m_scr[...]
        s = jnp.dot(ks, qt, preferred_element_type=F32)
        m_new = jnp.maximum(m, jnp.max(s, axis=0, keepdims=True))
        alpha = jnp.exp2(m - m_new)
        p = jnp.exp2(s - m_new)
        l_scr[...] = alpha * l_scr[...] + jnp.sum(p, axis=0, keepdims=True)
        acc_scr[...] = alpha * acc_scr[...] + jnp.dot(vts, p.astype(BF16), preferred_element_type=F32)
        m_scr[...] = m_new

    def one_pass(c, carry):
        off = pl.multiple_of(c * tk, tk)
        m = m_scr[...]
        s = jnp.dot(k_scr[pl.ds(off, tk), :], qt, preferred_element_type=F32)
        p = jnp.exp2(s - m)
        cmax = jnp.max(s, axis=0, keepdims=True)
        psum = jnp.sum(p, axis=0, keepdims=True)
        pv = jnp.dot(vt_scr[:, pl.ds(off, tk)], p.astype(BF16), preferred_element_type=F32)
        safe = jnp.max(cmax - m) <= ATTN_MAX_RISE

        @pl.when(safe)
        def _():
            m_new = jnp.maximum(m, cmax)
            alpha = jnp.exp2(m - m_new)
            l_scr[...] = (l_scr[...] + psum) * alpha
            acc_scr[...] = (acc_scr[...] + pv) * alpha
            m_scr[...] = m_new

        @pl.when(jnp.logical_not(safe))
        def _():
            two_pass(k_scr[pl.ds(off, tk), :], vt_scr[:, pl.ds(off, tk)])

        return carry

    m_scr[...] = jnp.full_like(m_scr, -jnp.inf)
    l_scr[...] = jnp.zeros_like(l_scr)
    acc_scr[...] = jnp.zeros_like(acc_scr)
    if tail:
        lo = n_main * tk
        two_pass(k_scr[lo:lo + tail, :], vt_scr[:, lo:lo + tail])
        first = 0
    else:
        two_pass(k_scr[0:tk, :], vt_scr[:, 0:tk])
        first = 1
    if n_main > first:
        lax.fori_loop(first, n_main, one_pass, 0)
    o_ref[...] = (acc_scr[...] / l_scr[...]).T.astype(o_ref.dtype)


def _mla_attention(qn, qr, kv, kpe, *, q_rows, k_rows, tq, name):
    hh = MLA_HEADS
    q0, nq = q_rows
    k0, nkeys = k_rows
    assert q0 % tq == 0 and nq % tq == 0 and k0 % nkeys == 0 and nkeys % 256 == 0
    tk = min(ATTN_K_ROWS, nkeys)
    n_main = nkeys // tk
    tail = nkeys - n_main * tk
    body = functools.partial(_mla_attn_body, scale=MLA_QK_DIM ** -0.5, tk=tk, n_main=n_main, tail=tail)
    kb = k0 // nkeys
    once = pl.Buffered(1)
    return pl.pallas_call(
        body, grid=(hh, nq // tq),
        in_specs=[pl.BlockSpec((tq, MLA_NOPE_DIM), lambda h, i: (q0 // tq + i, h)),
                  pl.BlockSpec((tq, LANES), lambda h, i: (q0 // tq + i, h // 2)),
                  pl.BlockSpec((nkeys, MLA_NOPE_DIM), lambda h, i: (kb, 2 * h), pipeline_mode=once),
                  pl.BlockSpec((nkeys, LANES), lambda h, i: (kb, 0), pipeline_mode=once),
                  pl.BlockSpec((nkeys, MLA_V_DIM), lambda h, i: (kb, 2 * h + 1), pipeline_mode=once)],
        out_specs=pl.BlockSpec((tq, MLA_V_DIM), lambda h, i: (i, h)),
        out_shape=jax.ShapeDtypeStruct((nq, hh * MLA_V_DIM), BF16),
        scratch_shapes=[pltpu.VMEM((nkeys, MLA_QK_DIM), BF16), pltpu.VMEM((MLA_V_DIM, nkeys), BF16),
                        pltpu.VMEM((1, tq), F32), pltpu.VMEM((1, tq), F32), pltpu.VMEM((MLA_V_DIM, tq), F32)],
        name=name, compiler_params=_params(("parallel", "arbitrary")),
    )(qn, qr, kv, kpe, kv)


def _mla_mixer(h, p, m, rope_c, rope_s, s_lat, with_ctx):
    t, d = h.shape
    n_ctx = t - s_lat
    hh = MLA_HEADS
    tm = _row_tile(t, MM_ROWS)
    cq = _matmul(h, p["mla_w_dq"][m].astype(BF16), out_dtype=BF16, tn=p["mla_w_dq"].shape[2],
                 epi=_epi_rmsnorm, epi_ops=[_row_vec_op(p["mla_q_norm"][m], p["mla_w_dq"].shape[2])], name="mla_dq")
    w_dkv = p["mla_w_dkv"][m]
    ckv = _matmul(h, w_dkv[:, :MLA_KV_RANK].astype(BF16), out_dtype=BF16, tn=MLA_KV_RANK,
                  epi=_epi_rmsnorm, epi_ops=[_row_vec_op(p["mla_kv_norm"][m], MLA_KV_RANK)], name="mla_dkv")
    w_pe = w_dkv[:, MLA_KV_RANK:].astype(BF16)
    kpe = _matmul(h, jnp.concatenate([w_pe, w_pe], axis=1), out_dtype=BF16, tn=LANES,
                  epi=_epi_rope, epi_ops=_rope_ops(rope_c, rope_s, tm), name="mla_kpe")
    w_uq = p["mla_w_uq"][m].astype(BF16).reshape(-1, hh, MLA_QK_DIM)
    qn = _matmul(cq, w_uq[:, :, :MLA_NOPE_DIM].reshape(-1, hh * MLA_NOPE_DIM), out_dtype=BF16, name="mla_uq_nope")
    qr = _matmul(cq, w_uq[:, :, MLA_NOPE_DIM:].reshape(-1, hh * MLA_ROPE_DIM), out_dtype=BF16,
                 epi=_epi_rope, epi_ops=_rope_ops(rope_c, rope_s, tm), name="mla_uq_rope")
    kv = _matmul(ckv, p["mla_w_ukv"][m].astype(BF16), out_dtype=BF16, name="mla_ukv")
    tq = ATTN_Q_ROWS if s_lat % ATTN_Q_ROWS == 0 else n_ctx
    o = _mla_attention(qn, qr, kv, kpe, q_rows=(0, s_lat), k_rows=(0, t), tq=tq, name="mla_attn_lat")
    if with_ctx:
        o_c = _mla_attention(qn, qr, kv, kpe, q_rows=(s_lat, n_ctx), k_rows=(s_lat, n_ctx), tq=n_ctx,
                             name="mla_attn_ctx")
        o = jnp.concatenate([o, o_c], axis=0)
    else:
        o = jnp.concatenate([o, jnp.zeros((n_ctx, o.shape[1]), o.dtype)], axis=0)
    return _matmul(o, p["mla_w_o"][m].astype(BF16), out_dtype=BF16, name="mla_o")


def _swa_body(sink_ref, q_ref, kp_ref, kc_ref, kn_ref, kx_ref, vp_ref, vc_ref, vn_ref, vx_ref, o_ref,
              *, scale, nb, grp, latent):
    j = pl.program_id(0)
    kh = pl.program_id(1)
    qb = q_ref[...]
    blk = qb.shape[0]
    dh = SWA_HEAD_DIM
    if latent:
        keys = jnp.concatenate([kx_ref[0], kp_ref[0], kc_ref[0], kn_ref[0]], axis=0)
        vals = jnp.concatenate([vx_ref[0], vp_ref[0], vc_ref[0], vn_ref[0]], axis=0)
        n_ctx = kx_ref.shape[1]
        nkeys = keys.shape[0]
        r = lax.broadcasted_iota(jnp.int32, (blk, nkeys), 0)
        c = lax.broadcasted_iota(jnp.int32, (blk, nkeys), 1) - n_ctx
        rel = c - blk - r
        kpos = (j - 1) * blk + c
        ok = (c < 0) | ((jnp.abs(rel) <= SWA_WINDOW) & (kpos >= 0) & (kpos < nb * blk))
    else:
        keys, vals = kx_ref[0], vx_ref[0]
        ok = None
    outs = []
    for g in range(grp):
        qg = (qb[:, g * dh:(g + 1) * dh].astype(F32) * scale).astype(BF16)
        s = lax.dot_general(qg, keys, (((1,), (1,)), ((), ())), preferred_element_type=F32)
        if ok is not None:
            s = jnp.where(ok, s, -jnp.inf)
        sink = sink_ref[kh * grp + g]
        m = jnp.maximum(jnp.max(s, axis=1, keepdims=True), sink)
        pr = jnp.exp(s - m)
        denom = jnp.sum(pr, axis=1, keepdims=True) + jnp.exp(sink - m)
        og = jnp.dot(pr.astype(BF16), vals, preferred_element_type=F32) / denom
        outs.append(og)
    o_ref[...] = jnp.concatenate(outs, axis=1).astype(o_ref.dtype)


def _swa_attention(q, k, v, sink, s_lat, latent):
    t = q.shape[0]
    n_ctx = t - s_lat
    grp = SWA_HEADS // SWA_KV_HEADS
    blk = SWA_BLOCK
    dh = SWA_HEAD_DIM
    width = grp * dh
    if latent:
        nb = s_lat // blk
        q0 = 0
    else:
        nb = 1
        blk = n_ctx
        q0 = s_lat // n_ctx
    ctx_blk = s_lat // n_ctx

    def kv_spec(shift):
        return pl.BlockSpec((1, SWA_BLOCK, dh), lambda j, kh: (kh, jnp.clip(j + shift, 0, max(nb - 1, 0)), 0))

    ctx_spec = pl.BlockSpec((1, n_ctx, dh), lambda j, kh: (kh, ctx_blk, 0))
    body = functools.partial(_swa_body, scale=dh ** -0.5, nb=nb, grp=grp, latent=latent)
    win = [kv_spec(-1), kv_spec(0), kv_spec(1)]
    return pl.pallas_call(
        body, grid=(nb, SWA_KV_HEADS),
        in_specs=[pl.BlockSpec(memory_space=pltpu.SMEM),
                  pl.BlockSpec((blk, width), lambda j, kh: (q0 + j, kh))] + win + [ctx_spec] + win + [ctx_spec],
        out_specs=pl.BlockSpec((blk, width), lambda j, kh: (j, kh)),
        out_shape=jax.ShapeDtypeStruct((nb * blk, SWA_HEADS * dh), BF16),
        name="swa_attn_lat" if latent else "swa_attn_ctx",
        compiler_params=_params(("parallel", "parallel")),
    )(sink.astype(F32), q, k, k, k, k, v, v, v, v)


def _swa_mixer(h, p, m, rope_c, rope_s, s_lat, with_ctx):
    t, d = h.shape
    n_ctx = t - s_lat
    tm = _row_tile(t, MM_ROWS)
    qw = SWA_HEADS * SWA_HEAD_DIM
    kw = SWA_KV_HEADS * SWA_HEAD_DIM
    w = p["swa_w_qkv"][m].astype(BF16)
    b = p["swa_b_qkv"][m]
    tn = 512
    qk = _matmul(h, w[:, :qw + kw], out_dtype=BF16, tn=tn, epi=_epi_bias_rope,
                 epi_ops=[_row_vec_op(b[:qw + kw], tn)] + _rope_ops(rope_c, rope_s, tm), name="swa_qk")
    vv = _matmul(h, w[:, qw + kw:], out_dtype=BF16, tn=tn, epi=_epi_bias,
                 epi_ops=[_row_vec_op(b[qw + kw:], tn)], name="swa_v")
    q = qk[:, :qw]
    k = qk[:, qw:].reshape(t, SWA_KV_HEADS, SWA_HEAD_DIM).transpose(1, 0, 2)
    v = vv.reshape(t, SWA_KV_HEADS, SWA_HEAD_DIM).transpose(1, 0, 2)
    sink = p["swa_sink"][m]
    o = _swa_attention(q, k, v, sink, s_lat, True)
    if with_ctx:
        o = jnp.concatenate([o, _swa_attention(q, k, v, sink, s_lat, False)], axis=0)
    else:
        o = jnp.concatenate([o, jnp.zeros((n_ctx, o.shape[1]), o.dtype)], axis=0)
    return _matmul(o, p["swa_w_o"][m].astype(BF16), out_dtype=BF16, epi=_epi_bias,
                   epi_ops=[_row_vec_op(p["swa_b_o"][m], 512)], tn=512, name="swa_o")


def _perm_cols(w):
    lead = w.shape[:-1]
    nh = w.shape[-1] // RWKV_HEAD
    return jnp.swapaxes(w.reshape(lead + (nh, RWKV_HEAD)), -1, -2).reshape(lead + (nh * RWKV_HEAD,))


def _rwkv_mix_body(h_ref, hp_ref, hn_ref, mu_ref, *out_refs, nb_lat, nb):
    i = pl.program_id(0)
    h = h_ref[...].astype(F32)
    tm = h.shape[0]
    first = (i == 0) | (i == nb_lat)
    last = (i == nb_lat - 1) | (i == nb - 1)
    prev_row = jnp.where(first, 0.0, hp_ref[SUBLANES - 1:SUBLANES, :].astype(F32))
    next_row = jnp.where(last, 0.0, hn_ref[0:1, :].astype(F32))
    rid = lax.broadcasted_iota(jnp.int32, (tm, 1), 0)
    prev = jnp.where(rid == 0, prev_row, pltpu.roll(h, 1, 0))
    nxt = jnp.where(rid == tm - 1, next_row, pltpu.roll(h, tm - 1, 0))
    xx = 0.5 * (prev + nxt) - h
    for n, o_ref in enumerate(out_refs):
        o_ref[...] = (h + xx * mu_ref[n:n + 1, :]).astype(o_ref.dtype)


def _rwkv_mix(h, mu, s_lat):
    t, d = h.shape
    tm = 128
    nb, nb_lat = t // tm, s_lat // tm
    r8 = tm // SUBLANES
    body = functools.partial(_rwkv_mix_body, nb_lat=nb_lat, nb=nb)
    row = pl.BlockSpec((tm, d), lambda i: (i, 0))
    return pl.pallas_call(
        body, grid=(nb,),
        in_specs=[row,
                  pl.BlockSpec((SUBLANES, d), lambda i: (jnp.maximum(i * r8 - 1, 0), 0)),
                  pl.BlockSpec((SUBLANES, d), lambda i: (jnp.minimum((i + 1) * r8, t // SUBLANES - 1), 0)),
                  pl.BlockSpec((6, d), lambda i: (0, 0))],
        out_specs=[row] * 6, out_shape=[jax.ShapeDtypeStruct((t, d), BF16)] * 6, name="rwkv_mix",
        compiler_params=_params(("parallel",)),
    )(h, h, h, mu)


def _head_sum(ref_or_val_slabs):
    acc = ref_or_val_slabs[0]
    for s in ref_or_val_slabs[1:]:
        acc = acc + s
    return acc + pltpu.roll(acc, 64, 1)


def _rwkv_prep_body(r_ref, k_ref, v_ref, wl0_ref, wl1_ref, al0_ref, al1_ref, kk_ref, ka_ref, rk_ref,
                    d0_ref, d1_ref, kz0_ref, kz1_ref, b0_ref, b1_ref, na_ref, bonus_ref, vdup_ref):
    nsl = r_ref.shape[1] // LANES
    sl = lambda ref, j: ref[:, j * LANES:(j + 1) * LANES]
    kks = [sl(k_ref, j) * sl(kk_ref, j) for j in range(nsl)]
    ss = _head_sum([x * x for x in kks])
    inv = lax.rsqrt(jnp.maximum(ss, 1e-24))
    bon = [None, None]
    for z, (wl_ref, al_ref, d_ref, kz_ref, b_ref) in enumerate(
            ((wl0_ref, al0_ref, d0_ref, kz0_ref, b0_ref), (wl1_ref, al1_ref, d1_ref, kz1_ref, b1_ref))):
        prods = []
        for j in range(nsl):
            kkn = kks[j] * inv
            if z == 0:
                na_ref[:, j * LANES:(j + 1) * LANES] = -kkn
            logw = -jax.nn.softplus(-sl(wl_ref, j)) - 0.5
            d_ref[:, j * LANES:(j + 1) * LANES] = jnp.exp(-jnp.exp(logw))
            a = jax.nn.sigmoid(sl(al_ref, j))
            kz = sl(k_ref, j) * (1.0 + (a - 1.0) * sl(ka_ref, j))
            kz_ref[:, j * LANES:(j + 1) * LANES] = kz
            b_ref[:, j * LANES:(j + 1) * LANES] = kkn * a
            prods.append(sl(r_ref, j) * kz * sl(rk_ref, j))
        bon[z] = _head_sum(prods)
    tot = bon[0] + bon[1]
    lane = lax.broadcasted_iota(jnp.int32, (1, LANES), 1)
    for j in range(nsl):
        vs = sl(v_ref, j)
        bonus_ref[:, j * LANES:(j + 1) * LANES] = tot * vs
        sw = pltpu.roll(vs, 64, 1)
        vdup_ref[:, (2 * j) * LANES:(2 * j + 1) * LANES] = jnp.where(lane < 64, vs, sw)
        vdup_ref[:, (2 * j + 1) * LANES:(2 * j + 2) * LANES] = jnp.where(lane < 64, sw, vs)


def _rwkv_prep(r, k, v, wl, al, k_k, k_a, r_k):
    t, d = r.shape
    tm = 64
    row = pl.BlockSpec((tm, d), lambda i: (i, 0))
    row2 = pl.BlockSpec((tm, 2 * d), lambda i: (i, 0))
    vec = pl.BlockSpec((1, d), lambda i: (0, 0))
    return pl.pallas_call(
        _rwkv_prep_body, grid=(t // tm,),
        in_specs=[row] * 7 + [vec] * 3,
        out_specs=[row] * 8 + [row2],
        out_shape=[jax.ShapeDtypeStruct((t, d), F32)] * 8 + [jax.ShapeDtypeStruct((t, 2 * d), F32)],
        name="rwkv_prep", compiler_params=_params(("parallel",)),
    )(r, k, v, wl[0], wl[1], al[0], al[1], k_k.reshape(1, d), k_a.reshape(1, d), r_k.reshape(1, d))


def _wkv_body(r_ref, w_ref, k_ref, a_ref, b_ref, v_ref, y_ref, s_ref, *, tb, reverse):
    @pl.when(pl.program_id(0) == 0)
    def _():
        s_ref[...] = jnp.zeros_like(s_ref)

    nv = s_ref.shape[0]

    def fold(x):
        rows = jnp.sum(x, axis=0, keepdims=True)
        return rows + pltpu.roll(rows, 64, 1)

    def step(i, carry):
        t = (tb - 1 - i) if reverse else i
        a2, w2, b2, k2, r2 = a_ref[t], w_ref[t], b_ref[t], k_ref[t], r_ref[t]
        for v in range(nv):
            st = s_ref[v]
            vrow = v_ref[t, pl.ds(v, 1), :]
            sa = fold(st * a2)
            sn = st * w2 + sa * b2 + vrow * k2
            s_ref[v] = sn
            y_ref[t, pl.ds(v, 1), :] = fold(sn * r2)
        return carry

    lax.fori_loop(0, tb, step, 0)


def _wkv_scan(r, w, k, a, b, vdup, s_lat, reverse):
    t, d = r.shape
    n = RWKV_HEAD
    nh = d // n
    tb = 64
    nb, nb_lat = t // tb, s_lat // tb
    nb_ctx = nb - nb_lat
    if reverse:
        order = lambda i: nb - 1 - i
    else:
        order = lambda i: jnp.where(i < nb_ctx, i + nb_lat, i - nb_ctx)
    kspec = pl.BlockSpec((tb, n // 2, 2 * nh), lambda i: (order(i), 0, 0))
    vspec = pl.BlockSpec((tb, n, 2 * nh), lambda i: (order(i), 0, 0))
    as3 = lambda z: z.reshape(t, n // 2, 2 * nh)
    body = functools.partial(_wkv_body, tb=tb, reverse=reverse)
    y = pl.pallas_call(
        body, grid=(nb,),
        in_specs=[kspec] * 5 + [vspec], out_specs=vspec,
        out_shape=jax.ShapeDtypeStruct((t, n, 2 * nh), F32),
        scratch_shapes=[pltpu.VMEM((n, n // 2, 2 * nh), F32)],
        name="wkv_rev" if reverse else "wkv_fwd",
        compiler_params=_params(("arbitrary",)),
    )(as3(r), as3(w), as3(k), as3(a), as3(b), vdup.reshape(t, n, 2 * nh))
    return y.reshape(t, 2 * d)


def _rwkv_finish_body(y0_ref, y1_ref, bonus_ref, g_ref, lw_ref, lb_ref, o_ref):
    nsl = bonus_ref.shape[1] // LANES
    lane = lax.broadcasted_iota(jnp.int32, (1, LANES), 1)
    ysl = lambda ref, j: ref[:, j * LANES:(j + 1) * LANES]
    ys = [jnp.where(lane < 64, ysl(y0_ref, 2 * j) + ysl(y1_ref, 2 * j), ysl(y0_ref, 2 * j + 1) + ysl(y1_ref, 2 * j + 1))
          for j in range(nsl)]
    mean = _head_sum(ys) * (1.0 / RWKV_HEAD)
    cs = [y - mean for y in ys]
    var = _head_sum([c * c for c in cs]) * (1.0 / RWKV_HEAD)
    inv = lax.rsqrt(var + RWKV_GN_EPS)
    for j in range(nsl):
        s = slice(j * LANES, (j + 1) * LANES)
        out = cs[j] * inv * lw_ref[:, s] + lb_ref[:, s] + bonus_ref[:, s]
        o_ref[:, s] = (out * g_ref[:, s].astype(F32)).astype(o_ref.dtype)


def _rwkv_finish(y0, y1, bonus, g, ln_w, ln_b):
    t, d = bonus.shape
    tm = 128
    row = pl.BlockSpec((tm, d), lambda i: (i, 0))
    row2 = pl.BlockSpec((tm, 2 * d), lambda i: (i, 0))
    vec = pl.BlockSpec((1, d), lambda i: (0, 0))
    return pl.pallas_call(
        _rwkv_finish_body, grid=(t // tm,), in_specs=[row2] * 2 + [row] * 2 + [vec] * 2, out_specs=row,
        out_shape=jax.ShapeDtypeStruct((t, d), BF16), name="rwkv_finish",
        compiler_params=_params(("parallel",)),
    )(y0, y1, bonus, g, ln_w.reshape(1, d), ln_b.reshape(1, d))


def _rwkv_mixer(h, p, m, s_lat):
    t, d = h.shape
    mixes = _rwkv_mix(h, p["rwkv_mu"][m], s_lat)
    w_rkv = _perm_cols(p["rwkv_w_rkv"][m]).astype(BF16)
    r = _matmul(mixes[0], w_rkv[0], out_dtype=F32, name="rwkv_r")
    k = _matmul(mixes[1], w_rkv[1], out_dtype=F32, name="rwkv_k")
    v = _matmul(mixes[2], w_rkv[2], out_dtype=F32, name="rwkv_v")
    lora = p["rwkv_w1"].shape[-1]
    w1 = jnp.concatenate([p["rwkv_w1"][m, 0], p["rwkv_w1"][m, 1]], axis=1).astype(BF16)
    a1 = jnp.concatenate([p["rwkv_a1"][m, 0], p["rwkv_a1"][m, 1]], axis=1).astype(BF16)
    wmid = _matmul(mixes[3], w1, out_dtype=BF16, epi=functools.partial(_epi_act, act=jnp.tanh), name="rwkv_w1")
    amid = _matmul(mixes[4], a1, out_dtype=BF16, name="rwkv_a1")
    w2 = _perm_cols(p["rwkv_w2"][m]).astype(BF16)
    a2 = _perm_cols(p["rwkv_a2"][m]).astype(BF16)
    w0 = _perm_cols(p["rwkv_w0"][m])
    a0 = _perm_cols(p["rwkv_a0"][m])
    wl, al = [], []
    for z in range(2):
        wl.append(_matmul(wmid[:, z * lora:(z + 1) * lora], w2[z], out_dtype=F32, epi=_epi_bias,
                          epi_ops=[_row_vec_op(w0[z], 1024)], tn=1024, name="rwkv_w2"))
        al.append(_matmul(amid[:, z * lora:(z + 1) * lora], a2[z], out_dtype=F32, epi=_epi_bias,
                          epi_ops=[_row_vec_op(a0[z], 1024)], tn=1024, name="rwkv_a2"))
    gmid = _matmul(mixes[5], p["rwkv_g1"][m].astype(BF16), out_dtype=BF16,
                   epi=functools.partial(_epi_act, act=jax.nn.sigmoid), name="rwkv_g1")
    g = _matmul(gmid, _perm_cols(p["rwkv_g2"][m]).astype(BF16), out_dtype=BF16, name="rwkv_g2")
    d0, d1, kz0, kz1, b0, b1, na, bonus, vdup = _rwkv_prep(
        r, k, v, wl, al, _perm_cols(p["rwkv_k_k"][m]), _perm_cols(p["rwkv_k_a"][m]), _perm_cols(p["rwkv_r_k"][m]))
    y0 = _wkv_scan(r, d0, kz0, na, b0, vdup, s_lat, False)
    y1 = _wkv_scan(r, d1, kz1, na, b1, vdup, s_lat, True)
    out = _rwkv_finish(y0, y1, bonus, g, _perm_cols(p["rwkv_ln_w"][m]), _perm_cols(p["rwkv_ln_b"][m]))
    w_o = p["rwkv_w_o"][m]
    nh = d // RWKV_HEAD
    w_o = w_o.reshape(nh, RWKV_HEAD, d).transpose(1, 0, 2).reshape(d, d).astype(BF16)
    return _matmul(out, w_o, out_dtype=BF16, name="rwkv_o")


def _epi_moe_act(acc, ids, b1_ref, gates_ref):
    e = ids[1]
    u = acc + b1_ref[0]
    x_glu = jnp.minimum(u[:, :EXPERT_FF], SWIGLU_LIMIT)
    x_lin = jnp.clip(u[:, EXPERT_FF:], -SWIGLU_LIMIT, SWIGLU_LIMIT)
    act = x_glu * jax.nn.sigmoid(SWIGLU_ALPHA * x_glu) * (x_lin + 1.0)
    gates = gates_ref[...]
    lane = lax.broadcasted_iota(jnp.int32, gates.shape, 1)
    gcol = jnp.sum(jnp.where(lane == e, gates, 0.0), axis=1, keepdims=True)
    return act * gcol


def _epi_moe_out(acc, ids, gates_ref, b2_ref):
    return acc + jnp.dot(gates_ref[...], b2_ref[...], preferred_element_type=F32)


def _moe(h2, gates, w1, b1, w2, b2):
    t, d = h2.shape
    e, _, f2 = w1.shape
    ff = f2 // 2
    tm = _row_tile(t, MM_ROWS)
    gate_spec = pl.BlockSpec((tm, e), lambda i, j, k: (i, 0))
    act = _matmul(
        h2, w1.astype(BF16), out_dtype=BF16, tm=tm, tn=f2, tk=d, n_total=e * f2,
        w_spec=pl.BlockSpec((None, d, f2), lambda i, j, k: (j, 0, 0)),
        epi=_epi_moe_act,
        epi_ops=[(b1.reshape(e, 1, f2), pl.BlockSpec((1, 1, f2), lambda i, j, k: (j, 0, 0))), (gates, gate_spec)],
        out_shape=jax.ShapeDtypeStruct((t, e * ff), BF16),
        out_spec=pl.BlockSpec((tm, ff), lambda i, j, k: (i, j)), name="moe_up")
    tn = 512
    return _matmul(
        act, w2.astype(BF16).reshape(e * ff, d), out_dtype=BF16, tm=tm, tn=tn, epi=_epi_moe_out,
        epi_ops=[(gates, gate_spec), (b2, pl.BlockSpec((e, tn), lambda i, j, k: (0, j)))], name="moe_down")


def kernel(x, c, ctx, c_ctx, ada_a, ada_b, ada_bias, norm_g, mla_w_dq, mla_q_norm, mla_w_uq, mla_w_dkv, mla_kv_norm, mla_w_ukv, mla_w_o, rwkv_mu, rwkv_w_rkv, rwkv_w0, rwkv_w1, rwkv_w2, rwkv_a0, rwkv_a1, rwkv_a2, rwkv_g1, rwkv_g2, rwkv_k_k, rwkv_k_a, rwkv_r_k, rwkv_ln_w, rwkv_ln_b, rwkv_w_o, swa_w_qkv, swa_b_qkv, swa_sink, swa_w_o, swa_b_o, moe_w_router, moe_b_router, moe_w1, moe_b1, moe_w2, moe_b2):
    p = dict(mla_w_dq=mla_w_dq, mla_q_norm=mla_q_norm, mla_w_uq=mla_w_uq, mla_w_dkv=mla_w_dkv,
             mla_kv_norm=mla_kv_norm, mla_w_ukv=mla_w_ukv, mla_w_o=mla_w_o,
             rwkv_mu=rwkv_mu, rwkv_w_rkv=rwkv_w_rkv, rwkv_w0=rwkv_w0, rwkv_w1=rwkv_w1, rwkv_w2=rwkv_w2,
             rwkv_a0=rwkv_a0, rwkv_a1=rwkv_a1, rwkv_a2=rwkv_a2, rwkv_g1=rwkv_g1, rwkv_g2=rwkv_g2,
             rwkv_k_k=rwkv_k_k, rwkv_k_a=rwkv_k_a, rwkv_r_k=rwkv_r_k, rwkv_ln_w=rwkv_ln_w,
             rwkv_ln_b=rwkv_ln_b, rwkv_w_o=rwkv_w_o, swa_w_qkv=swa_w_qkv, swa_b_qkv=swa_b_qkv,
             swa_sink=swa_sink, swa_w_o=swa_w_o, swa_b_o=swa_b_o)
    batch, s_lat, d = x.shape
    n_ctx = ctx.shape[1]
    assert batch == 1 and s_lat % n_ctx == 0 and n_ctx % EW_ROWS == 0
    depth = ada_a.shape[0]
    xs = jnp.concatenate([x[0], ctx[0]], axis=0)
    cond8 = jnp.concatenate([c, c_ctx[None, :], jnp.zeros((SUBLANES - 2, d), F32)], axis=0)
    rope_c, rope_s = _rope_tables(s_lat, n_ctx)
    mods = [_adaln(cond8, ada_a[i], ada_b[i], ada_bias[i]) for i in range(depth)]
    h = _norm_mod(xs, norm_g[0, 0], mods[0], s_lat)
    for i in range(depth):
        last = i == depth - 1
        kind, m = i % N_MIXERS, i // N_MIXERS
        if kind == 0:
            y = _mla_mixer(h, p, m, rope_c, rope_s, s_lat, not last)
        elif kind == 1:
            y = _rwkv_mixer(h, p, m, s_lat)
        else:
            y = _swa_mixer(h, p, m, rope_c, rope_s, s_lat, not last)
        xs, h2, gates = _resid_norm(xs, y, norm_g[i, 1], mods[i], 2, s_lat, gb=norm_g[i, 2], mods_b=mods[i],
                                    mod_rows=(3, 4), router=(moe_w_router[i], moe_b_router[i]))
        f = _moe(h2, gates, moe_w1[i], moe_b1[i], moe_w2[i], moe_b2[i])
        if last:
            (xs,) = _resid_norm(xs, f, norm_g[i, 3], mods[i], 5, s_lat)
        else:
            xs, h = _resid_norm(xs, f, norm_g[i, 3], mods[i], 5, s_lat, gb=norm_g[i + 1, 0],
                                mods_b=mods[i + 1], mod_rows=(0, 1))
    return xs[:s_lat][None]
```

```python
import functools

import jax
import jax.numpy as jnp
from jax import lax
from jax.experimental import pallas as pl
from jax.experimental.pallas import tpu as pltpu

F32 = jnp.float32
BF16 = jnp.bfloat16

GRID_W = 64
ROPE_BASE = 10000.0
NORM_EPS = 1e-6
MLA_HEADS = 64
MLA_KV_RANK = 512
MLA_NOPE_DIM = 128
MLA_ROPE_DIM = 64
MLA_V_DIM = 128
MLA_QK_DIM = MLA_NOPE_DIM + MLA_ROPE_DIM
RWKV_HEAD = 64
RWKV_GN_EPS = 64e-5
SWA_HEADS = 64
SWA_KV_HEADS = 8
SWA_HEAD_DIM = 64
SWA_WINDOW = 128
SWA_BLOCK = 128
N_EXPERTS = 32
TOP_K = 4
EXPERT_FF = 384
SWIGLU_ALPHA = 1.702
SWIGLU_LIMIT = 7.0
N_MIXERS = 3

LANES = 128
SUBLANES = 8
VMEM_LIMIT = 56 * 1024 * 1024
LOG2E = 1.4426950408889634

MM_ROWS = 1280
MM_K = 4096
EW_ROWS = 256
ATTN_Q_ROWS = 2048
ATTN_K_ROWS = 1024
ATTN_MAX_RISE = 64.0


def _params(sem):
    return pltpu.CompilerParams(dimension_semantics=sem, vmem_limit_bytes=VMEM_LIMIT)


def _row_tile(t, pref):
    if t <= pref:
        return t
    best = None
    for cand in range(pref, 7, -8):
        if t % cand == 0:
            best = cand
            break
    assert best is not None, (t, pref)
    return best


def _col_tile(n, pref):
    if n <= pref:
        return n
    for cand in range(pref, LANES - 1, -LANES):
        if n % cand == 0:
            return cand
    return n


def _mm_body(*refs, nk, n_epi, epi, store):
    x_ref, w_ref = refs[0], refs[1]
    epi_refs = refs[2:2 + n_epi]
    o_ref = refs[2 + n_epi]
    ids = (pl.program_id(0), pl.program_id(1))
    part = jnp.dot(x_ref[...].astype(BF16), w_ref[...], preferred_element_type=F32)
    if nk == 1:
        store(o_ref, epi(part, ids, *epi_refs))
        return
    acc_ref = refs[3 + n_epi]
    k = pl.program_id(2)

    @pl.when(k == 0)
    def _():
        acc_ref[...] = part

    @pl.when(k > 0)
    def _():
        acc_ref[...] += part

    @pl.when(k == nk - 1)
    def _():
        store(o_ref, epi(acc_ref[...], ids, *epi_refs))


def _store_plain(o_ref, val):
    o_ref[...] = val.astype(o_ref.dtype)


def _epi_none(acc, ids):
    return acc


def _matmul(x, w, *, out_dtype, tm=None, tn=None, tk=None, epi=_epi_none, epi_ops=(),
            w_spec=None, n_total=None, out_shape=None, out_spec=None, store=_store_plain, name="mm"):
    m, kdim = x.shape
    n = n_total if n_total is not None else w.shape[1]
    tm = tm or _row_tile(m, MM_ROWS)
    tk = tk or _col_tile(kdim, MM_K)
    tn = tn or _col_tile(n, 1024 if tk <= 1024 else 512)
    assert m % tm == 0 and n % tn == 0 and kdim % tk == 0, (m, n, kdim, tm, tn, tk)
    nk = kdim // tk
    grid = (m // tm, n // tn, nk)
    if w_spec is None:
        w_spec = pl.BlockSpec((tk, tn), lambda i, j, k: (k, j))
    in_specs = [pl.BlockSpec((tm, tk), lambda i, j, k: (i, k)), w_spec] + [s for _, s in epi_ops]
    if out_shape is None:
        out_shape = jax.ShapeDtypeStruct((m, n), out_dtype)
        out_spec = pl.BlockSpec((tm, tn), lambda i, j, k: (i, j))
    scratch = [pltpu.VMEM((tm, tn), F32)] if nk > 1 else []
    body = functools.partial(_mm_body, nk=nk, n_epi=len(epi_ops), epi=epi, store=store)
    return pl.pallas_call(
        body, grid=grid, in_specs=in_specs, out_specs=out_spec, out_shape=out_shape,
        scratch_shapes=scratch, name=name,
        compiler_params=_params(("parallel", "parallel", "arbitrary")),
    )(x, w, *[a for a, _ in epi_ops])


def _row_vec_op(vec, tn):
    return (vec.reshape(1, -1).astype(F32), pl.BlockSpec((1, tn), lambda i, j, k: (0, j)))


def _epi_bias(acc, ids, b_ref):
    return acc + b_ref[...]


def _epi_act(acc, ids, *, act):
    return act(acc)


def _epi_rmsnorm(acc, ids, g_ref):
    ms = jnp.mean(acc * acc, axis=-1, keepdims=True)
    return acc * lax.rsqrt(ms + NORM_EPS) * g_ref[...]


def _rope_cols(x, c, sg):
    lane = lax.broadcasted_iota(jnp.int32, (1, LANES), 1)
    first_half = (lane % 64) < 32
    outs = []
    for g in range(x.shape[1] // LANES):
        xg = x[:, g * LANES:(g + 1) * LANES]
        partner = jnp.where(first_half, pltpu.roll(xg, LANES - 32, 1), pltpu.roll(xg, 32, 1))
        outs.append(xg * c + partner * sg)
    return outs[0] if len(outs) == 1 else jnp.concatenate(outs, axis=1)


def _epi_rope(acc, ids, c_ref, s_ref):
    return _rope_cols(acc, c_ref[...], s_ref[...])


def _epi_bias_rope(acc, ids, b_ref, c_ref, s_ref):
    return _rope_cols(acc + b_ref[...], c_ref[...], s_ref[...])


def _rope_ops(rope_c, rope_s, tm):
    spec = pl.BlockSpec((tm, LANES), lambda i, j, k: (i, 0))
    return [(rope_c, spec), (rope_s, spec)]


def _rope_tables(s_lat, n_ctx):
    t = jnp.arange(s_lat)
    row = (t // GRID_W).astype(F32)
    col = (t % GRID_W).astype(F32)
    axis_dim = 32
    inv = ROPE_BASE ** (-jnp.arange(0, axis_dim, 2, dtype=F32) / axis_dim)
    ang = jnp.concatenate([row[:, None] * inv, col[:, None] * inv], axis=-1)
    cos, sin = jnp.cos(ang), jnp.sin(ang)
    c64 = jnp.concatenate([cos, cos], axis=-1)
    s64 = jnp.concatenate([-sin, sin], axis=-1)
    c = jnp.concatenate([c64, c64], axis=-1)
    s = jnp.concatenate([s64, s64], axis=-1)
    c = jnp.concatenate([c, jnp.ones((n_ctx, LANES), F32)], axis=0)
    s = jnp.concatenate([s, jnp.zeros((n_ctx, LANES), F32)], axis=0)
    return c, s


def _rms(x, g):
    return x * lax.rsqrt(jnp.mean(x * x, axis=-1, keepdims=True) + NORM_EPS) * g


def _norm_mod_body(x_ref, g_ref, mod_ref, h_ref):
    h = _rms(x_ref[...], g_ref[...]) * (1.0 + mod_ref[0, 1:2, :]) + mod_ref[0, 0:1, :]
    h_ref[...] = h.astype(h_ref.dtype)


def _norm_mod(x, g, mods, s_lat):
    t, d = x.shape
    tm = EW_ROWS
    nb_lat = s_lat // tm
    return pl.pallas_call(
        _norm_mod_body, grid=(t // tm,),
        in_specs=[pl.BlockSpec((tm, d), lambda i: (i, 0)),
                  pl.BlockSpec((1, d), lambda i: (0, 0)),
                  pl.BlockSpec((1, 6, d), lambda i: (jnp.where(i >= nb_lat, 1, 0), 0, 0))],
        out_specs=pl.BlockSpec((tm, d), lambda i: (i, 0)),
        out_shape=jax.ShapeDtypeStruct((t, d), BF16), name="norm_mod",
        compiler_params=_params(("parallel",)),
    )(x, g.reshape(1, d), mods)


def _top4_gates(logits):
    lane = lax.broadcasted_iota(jnp.int32, logits.shape, 1)
    work = logits
    gate = jnp.zeros_like(logits)
    denom = jnp.zeros((logits.shape[0], 1), F32)
    m0 = None
    for _ in range(TOP_K):
        m = jnp.max(work, axis=1, keepdims=True)
        idx = jnp.min(jnp.where(work == m, lane, N_EXPERTS), axis=1, keepdims=True)
        hit = lane == idx
        if m0 is None:
            m0 = m
        wk = jnp.exp(m - m0)
        gate = gate + jnp.where(hit, wk, 0.0)
        denom = denom + wk
        work = jnp.where(hit, -jnp.inf, work)
    return gate / denom


def _resid_body(*refs, gate_row, mod_rows, router):
    x_ref, y_ref, ga_ref, moda_ref = refs[:4]
    pos = 4
    xn = x_ref[...] + moda_ref[0, gate_row:gate_row + 1, :] * _rms(y_ref[...].astype(F32), ga_ref[...])
    if mod_rows is None:
        refs[pos][...] = xn
        return
    gb_ref, modb_ref = refs[pos], refs[pos + 1]
    pos += 2
    if router:
        wr_ref, br_ref = refs[pos], refs[pos + 1]
        pos += 2
    xn_ref, h_ref = refs[pos], refs[pos + 1]
    xn_ref[...] = xn
    sh, sc = mod_rows
    h = _rms(xn, gb_ref[...]) * (1.0 + modb_ref[0, sc:sc + 1, :]) + modb_ref[0, sh:sh + 1, :]
    h_ref[...] = h.astype(h_ref.dtype)
    if router:
        gates_ref = refs[pos + 2]
        logits = jnp.dot(h, wr_ref[...], precision=lax.Precision.HIGHEST,
                         preferred_element_type=F32) + br_ref[...]
        gates_ref[...] = _top4_gates(logits)


def _resid_norm(x, y, ga, mods_a, gate_row, s_lat, gb=None, mods_b=None, mod_rows=None, router=None):
    t, d = x.shape
    tm = EW_ROWS
    nb_lat = s_lat // tm
    row = pl.BlockSpec((tm, d), lambda i: (i, 0))
    vec = pl.BlockSpec((1, d), lambda i: (0, 0))
    mod = pl.BlockSpec((1, 6, d), lambda i: (jnp.where(i >= nb_lat, 1, 0), 0, 0))
    ins = [x, y, ga.reshape(1, d), mods_a]
    in_specs = [row, row, vec, mod]
    out_shape = [jax.ShapeDtypeStruct((t, d), F32)]
    out_specs = [row]
    if mod_rows is not None:
        ins += [gb.reshape(1, d), mods_b]
        in_specs += [vec, mod]
        out_shape.append(jax.ShapeDtypeStruct((t, d), BF16))
        out_specs.append(row)
        if router is not None:
            wr, br = router
            ins += [wr, br.reshape(1, N_EXPERTS)]
            in_specs += [pl.BlockSpec((d, N_EXPERTS), lambda i: (0, 0)),
                         pl.BlockSpec((1, N_EXPERTS), lambda i: (0, 0))]
            out_shape.append(jax.ShapeDtypeStruct((t, N_EXPERTS), F32))
            out_specs.append(pl.BlockSpec((tm, N_EXPERTS), lambda i: (i, 0)))
    body = functools.partial(_resid_body, gate_row=gate_row, mod_rows=mod_rows, router=router is not None)
    outs = pl.pallas_call(
        body, grid=(t // tm,), in_specs=in_specs, out_specs=out_specs, out_shape=out_shape,
        name="resid_norm", compiler_params=_params(("parallel",)),
    )(*ins)
    return outs


def _silu(x):
    return x * jax.nn.sigmoid(x)


def _adaln(cond8, a, b, bias):
    d = cond8.shape[1]
    sil = _silu(cond8)
    low = _matmul(sil, a.astype(BF16), out_dtype=F32, tm=8, name="ada_a")
    m = _matmul(low, b.astype(BF16), out_dtype=F32, tm=8, epi=_epi_bias,
                epi_ops=[_row_vec_op(bias, 1024)], tn=1024, name="ada_b")
    return m[:2].reshape(2, 6, d)


def _mla_attn_body(qn_ref, qr_ref, kn_ref, kpe_ref, v_ref, o_ref, k_scr, vt_scr, m_scr, l_scr, acc_scr,
                   *, scale, tk, n_main, tail):
    hd = pl.program_id(0)
    nkeys = k_scr.shape[0]
    vchunk = 256

    @pl.when(pl.program_id(1) == 0)
    def _():
        k_scr[:, :MLA_NOPE_DIM] = kn_ref[...]
        k_scr[:, MLA_NOPE_DIM:] = kpe_ref[:, :MLA_ROPE_DIM]

        def tr(c, carry):
            off = pl.multiple_of(c * vchunk, vchunk)
            vt_scr[:, pl.ds(off, vchunk)] = v_ref[pl.ds(off, vchunk), :].astype(F32).T.astype(BF16)
            return carry

        lax.fori_loop(0, nkeys // vchunk, tr, 0)

    c0 = scale * LOG2E
    qn_t = qn_ref[...].astype(F32).T * c0
    qr_t = qr_ref[...].astype(F32).T * c0
    qr_t = jnp.where(hd % 2 == 0, qr_t[:MLA_ROPE_DIM], qr_t[MLA_ROPE_DIM:])
    qt = jnp.concatenate([qn_t, qr_t], axis=0).astype(BF16)

    def two_pass(ks, vts):
        m = m_scr[...]
        s = jnp.dot(ks, qt, preferred_element_type=F32)
        m_new = jnp.maximum(m, jnp.max(s, axis=0, keepdims=True))
        alpha = jnp.exp2(m - m_new)
        p = jnp.exp2(s - m_new)
        l_scr[...] = alpha * l_scr[...] + jnp.sum(p, axis=0, keepdims=True)
        acc_scr[...] = alpha * acc_scr[...] + jnp.dot(vts, p.astype(BF16), preferred_element_type=F32)
        m_scr[...] = m_new

    def one_pass(c, carry):
        off = pl.multiple_of(c * tk, tk)
        m = m_scr[...]
        s = jnp.dot(k_scr[pl.ds(off, tk), :], qt, preferred_element_type=F32)
        p = jnp.exp2(s - m)
        cmax = jnp.max(s, axis=0, keepdims=True)
        psum = jnp.sum(p, axis=0, keepdims=True)
        pv = jnp.dot(vt_scr[:, pl.ds(off, tk)], p.astype(BF16), preferred_element_type=F32)
        safe = jnp.max(cmax - m) <= ATTN_MAX_RISE

        @pl.when(safe)
        def _():
            m_new = jnp.maximum(m, cmax)
            alpha = jnp.exp2(m - m_new)
            l_scr[...] = (l_scr[...] + psum) * alpha
            acc_scr[...] = (acc_scr[...] + pv) * alpha
            m_scr[...] = m_new

        @pl.when(jnp.logical_not(safe))
        def _():
            two_pass(k_scr[pl.ds(off, tk), :], vt_scr[:, pl.ds(off, tk)])

        return carry

    m_scr[...] = jnp.full_like(m_scr, -jnp.inf)
    l_scr[...] = jnp.zeros_like(l_scr)
    acc_scr[...] = jnp.zeros_like(acc_scr)
    if tail:
        lo = n_main * tk
        two_pass(k_scr[lo:lo + tail, :], vt_scr[:, lo:lo + tail])
        first = 0
    else:
        two_pass(k_scr[0:tk, :], vt_scr[:, 0:tk])
        first = 1
    if n_main > first:
        lax.fori_loop(first, n_main, one_pass, 0)
    o_ref[...] = (acc_scr[...] / l_scr[...]).T.astype(o_ref.dtype)


def _mla_attention(qn, qr, kv, kpe, *, q_rows, k_rows, tq, name):
    hh = MLA_HEADS
    q0, nq = q_rows
    k0, nkeys = k_rows
    assert q0 % tq == 0 and nq % tq == 0 and k0 % nkeys == 0 and nkeys % 256 == 0
    tk = min(ATTN_K_ROWS, nkeys)
    n_main = nkeys // tk
    tail = nkeys - n_main * tk
    body = functools.partial(_mla_attn_body, scale=MLA_QK_DIM ** -0.5, tk=tk, n_main=n_main, tail=tail)
    kb = k0 // nkeys
    once = pl.Buffered(1)
    return pl.pallas_call(
        body, grid=(hh, nq // tq),
        in_specs=[pl.BlockSpec((tq, MLA_NOPE_DIM), lambda h, i: (q0 // tq + i, h)),
                  pl.BlockSpec((tq, LANES), lambda h, i: (q0 // tq + i, h // 2)),
                  pl.BlockSpec((nkeys, MLA_NOPE_DIM), lambda h, i: (kb, 2 * h), pipeline_mode=once),
                  pl.BlockSpec((nkeys, LANES), lambda h, i: (kb, 0), pipeline_mode=once),
                  pl.BlockSpec((nkeys, MLA_V_DIM), lambda h, i: (kb, 2 * h + 1), pipeline_mode=once)],
        out_specs=pl.BlockSpec((tq, MLA_V_DIM), lambda h, i: (i, h)),
        out_shape=jax.ShapeDtypeStruct((nq, hh * MLA_V_DIM), BF16),
        scratch_shapes=[pltpu.VMEM((nkeys, MLA_QK_DIM), BF16), pltpu.VMEM((MLA_V_DIM, nkeys), BF16),
                        pltpu.VMEM((1, tq), F32), pltpu.VMEM((1, tq), F32), pltpu.VMEM((MLA_V_DIM, tq), F32)],
        name=name, compiler_params=_params(("parallel", "arbitrary")),
    )(qn, qr, kv, kpe, kv)


def _mla_mixer(h, p, m, rope_c, rope_s, s_lat, with_ctx):
    t, d = h.shape
    n_ctx = t - s_lat
    hh = MLA_HEADS
    tm = _row_tile(t, MM_ROWS)
    cq = _matmul(h, p["mla_w_dq"][m].astype(BF16), out_dtype=BF16, tn=p["mla_w_dq"].shape[2],
                 epi=_epi_rmsnorm, epi_ops=[_row_vec_op(p["mla_q_norm"][m], p["mla_w_dq"].shape[2])], name="mla_dq")
    w_dkv = p["mla_w_dkv"][m]
    ckv = _matmul(h, w_dkv[:, :MLA_KV_RANK].astype(BF16), out_dtype=BF16, tn=MLA_KV_RANK,
                  epi=_epi_rmsnorm, epi_ops=[_row_vec_op(p["mla_kv_norm"][m], MLA_KV_RANK)], name="mla_dkv")
    w_pe = w_dkv[:, MLA_KV_RANK:].astype(BF16)
    kpe = _matmul(h, jnp.concatenate([w_pe, w_pe], axis=1), out_dtype=BF16, tn=LANES,
                  epi=_epi_rope, epi_ops=_rope_ops(rope_c, rope_s, tm), name="mla_kpe")
    w_uq = p["mla_w_uq"][m].astype(BF16).reshape(-1, hh, MLA_QK_DIM)
    qn = _matmul(cq, w_uq[:, :, :MLA_NOPE_DIM].reshape(-1, hh * MLA_NOPE_DIM), out_dtype=BF16, name="mla_uq_nope")
    qr = _matmul(cq, w_uq[:, :, MLA_NOPE_DIM:].reshape(-1, hh * MLA_ROPE_DIM), out_dtype=BF16,
                 epi=_epi_rope, epi_ops=_rope_ops(rope_c, rope_s, tm), name="mla_uq_rope")
    kv = _matmul(ckv, p["mla_w_ukv"][m].astype(BF16), out_dtype=BF16, name="mla_ukv")
    tq = ATTN_Q_ROWS if s_lat % ATTN_Q_ROWS == 0 else n_ctx
    o = _mla_attention(qn, qr, kv, kpe, q_rows=(0, s_lat), k_rows=(0, t), tq=tq, name="mla_attn_lat")
    if with_ctx:
        o_c = _mla_attention(qn, qr, kv, kpe, q_rows=(s_lat, n_ctx), k_rows=(s_lat, n_ctx), tq=n_ctx,
                             name="mla_attn_ctx")
        o = jnp.concatenate([o, o_c], axis=0)
    else:
        o = jnp.concatenate([o, jnp.zeros((n_ctx, o.shape[1]), o.dtype)], axis=0)
    return _matmul(o, p["mla_w_o"][m].astype(BF16), out_dtype=BF16, name="mla_o")


def _swa_body(sink_ref, q_ref, kp_ref, kc_ref, kn_ref, kx_ref, vp_ref, vc_ref, vn_ref, vx_ref, o_ref,
              *, scale, nb, grp, latent):
    j = pl.program_id(0)
    kh = pl.program_id(1)
    blk = q_ref.shape[0]
    dh = SWA_HEAD_DIM
    q_t = q_ref[...].astype(F32).T * (scale * LOG2E)
    qt = jnp.concatenate([q_t[g * dh:(g + 1) * dh] for g in range(grp)], axis=1).astype(BF16)
    if latent:
        keys = jnp.concatenate([kx_ref[0], kp_ref[0], kc_ref[0], kn_ref[0]], axis=0)
        vals = jnp.concatenate([vx_ref[0], vp_ref[0], vc_ref[0], vn_ref[0]], axis=0)
        n_ctx = kx_ref.shape[1]
        nkeys = keys.shape[0]
        c = lax.broadcasted_iota(jnp.int32, (nkeys, blk), 0) - n_ctx
        r = lax.broadcasted_iota(jnp.int32, (nkeys, blk), 1)
        rel = c - blk - r
        kpos = (j - 1) * blk + c
        ok1 = (c < 0) | ((jnp.abs(rel) <= SWA_WINDOW) & (kpos >= 0) & (kpos < nb * blk))
        ok = jnp.concatenate([ok1] * grp, axis=1)
    else:
        keys, vals = kx_ref[0], vx_ref[0]
        ok = None
    s = jnp.dot(keys, qt, preferred_element_type=F32)
    if ok is not None:
        s = jnp.where(ok, s, -jnp.inf)
    sink = jnp.concatenate([jnp.full((1, blk), sink_ref[kh * grp + g], F32) for g in range(grp)], axis=1) * LOG2E
    m = jnp.maximum(jnp.max(s, axis=0, keepdims=True), sink)
    pr = jnp.exp2(s - m)
    denom = jnp.sum(pr, axis=0, keepdims=True) + jnp.exp2(sink - m)
    o_t = jnp.dot(vals.astype(F32).T.astype(BF16), pr.astype(BF16), preferred_element_type=F32) / denom
    o_all = jnp.concatenate([o_t[:, g * blk:(g + 1) * blk] for g in range(grp)], axis=0)
    o_ref[...] = o_all.T.astype(o_ref.dtype)


def _swa_attention(q, k, v, sink, s_lat, latent):
    t = q.shape[0]
    n_ctx = t - s_lat
    grp = SWA_HEADS // SWA_KV_HEADS
    blk = SWA_BLOCK
    dh = SWA_HEAD_DIM
    width = grp * dh
    if latent:
        nb = s_lat // blk
        q0 = 0
    else:
        nb = 1
        blk = n_ctx
        q0 = s_lat // n_ctx
    ctx_blk = s_lat // n_ctx

    def kv_spec(shift):
        return pl.BlockSpec((1, SWA_BLOCK, dh), lambda j, kh: (kh, jnp.clip(j + shift, 0, max(nb - 1, 0)), 0))

    ctx_spec = pl.BlockSpec((1, n_ctx, dh), lambda j, kh: (kh, ctx_blk, 0))
    body = functools.partial(_swa_body, scale=dh ** -0.5, nb=nb, grp=grp, latent=latent)
    win = [kv_spec(-1), kv_spec(0), kv_spec(1)]
    return pl.pallas_call(
        body, grid=(nb, SWA_KV_HEADS),
        in_specs=[pl.BlockSpec(memory_space=pltpu.SMEM),
                  pl.BlockSpec((blk, width), lambda j, kh: (q0 + j, kh))] + win + [ctx_spec] + win + [ctx_spec],
        out_specs=pl.BlockSpec((blk, width), lambda j, kh: (j, kh)),
        out_shape=jax.ShapeDtypeStruct((nb * blk, SWA_HEADS * dh), BF16),
        name="swa_attn_lat" if latent else "swa_attn_ctx",
        compiler_params=_params(("parallel", "parallel")),
    )(sink.astype(F32), q, k, k, k, k, v, v, v, v)


def _swa_mixer(h, p, m, rope_c, rope_s, s_lat, with_ctx):
    t, d = h.shape
    n_ctx = t - s_lat
    tm = _row_tile(t, MM_ROWS)
    qw = SWA_HEADS * SWA_HEAD_DIM
    kw = SWA_KV_HEADS * SWA_HEAD_DIM
    w = p["swa_w_qkv"][m].astype(BF16)
    b = p["swa_b_qkv"][m]
    tn = 512
    qk = _matmul(h, w[:, :qw + kw], out_dtype=BF16, tn=tn, epi=_epi_bias_rope,
                 epi_ops=[_row_vec_op(b[:qw + kw], tn)] + _rope_ops(rope_c, rope_s, tm), name="swa_qk")
    vv = _matmul(h, w[:, qw + kw:], out_dtype=BF16, tn=tn, epi=_epi_bias,
                 epi_ops=[_row_vec_op(b[qw + kw:], tn)], name="swa_v")
    q = qk[:, :qw]
    k = qk[:, qw:].reshape(t, SWA_KV_HEADS, SWA_HEAD_DIM).transpose(1, 0, 2)
    v = vv.reshape(t, SWA_KV_HEADS, SWA_HEAD_DIM).transpose(1, 0, 2)
    sink = p["swa_sink"][m]
    o = _swa_attention(q, k, v, sink, s_lat, True)
    if with_ctx:
        o = jnp.concatenate([o, _swa_attention(q, k, v, sink, s_lat, False)], axis=0)
    else:
        o = jnp.concatenate([o, jnp.zeros((n_ctx, o.shape[1]), o.dtype)], axis=0)
    return _matmul(o, p["swa_w_o"][m].astype(BF16), out_dtype=BF16, epi=_epi_bias,
                   epi_ops=[_row_vec_op(p["swa_b_o"][m], 512)], tn=512, name="swa_o")


def _perm_cols(w):
    lead = w.shape[:-1]
    nh = w.shape[-1] // RWKV_HEAD
    return jnp.swapaxes(w.reshape(lead + (nh, RWKV_HEAD)), -1, -2).reshape(lead + (nh * RWKV_HEAD,))


def _rwkv_mix_body(h_ref, hp_ref, hn_ref, mu_ref, *out_refs, nb_lat, nb):
    i = pl.program_id(0)
    h = h_ref[...].astype(F32)
    tm = h.shape[0]
    first = (i == 0) | (i == nb_lat)
    last = (i == nb_lat - 1) | (i == nb - 1)
    prev_row = jnp.where(first, 0.0, hp_ref[SUBLANES - 1:SUBLANES, :].astype(F32))
    next_row = jnp.where(last, 0.0, hn_ref[0:1, :].astype(F32))
    rid = lax.broadcasted_iota(jnp.int32, (tm, 1), 0)
    prev = jnp.where(rid == 0, prev_row, pltpu.roll(h, 1, 0))
    nxt = jnp.where(rid == tm - 1, next_row, pltpu.roll(h, tm - 1, 0))
    xx = 0.5 * (prev + nxt) - h
    for n, o_ref in enumerate(out_refs):
        o_ref[...] = (h + xx * mu_ref[n:n + 1, :]).astype(o_ref.dtype)


def _rwkv_mix(h, mu, s_lat):
    t, d = h.shape
    tm = 128
    nb, nb_lat = t // tm, s_lat // tm
    r8 = tm // SUBLANES
    body = functools.partial(_rwkv_mix_body, nb_lat=nb_lat, nb=nb)
    row = pl.BlockSpec((tm, d), lambda i: (i, 0))
    return pl.pallas_call(
        body, grid=(nb,),
        in_specs=[row,
                  pl.BlockSpec((SUBLANES, d), lambda i: (jnp.maximum(i * r8 - 1, 0), 0)),
                  pl.BlockSpec((SUBLANES, d), lambda i: (jnp.minimum((i + 1) * r8, t // SUBLANES - 1), 0)),
                  pl.BlockSpec((6, d), lambda i: (0, 0))],
        out_specs=[row] * 6, out_shape=[jax.ShapeDtypeStruct((t, d), BF16)] * 6, name="rwkv_mix",
        compiler_params=_params(("parallel",)),
    )(h, h, h, mu)


def _head_sum(ref_or_val_slabs):
    acc = ref_or_val_slabs[0]
    for s in ref_or_val_slabs[1:]:
        acc = acc + s
    return acc + pltpu.roll(acc, 64, 1)


def _rwkv_prep_body(r_ref, k_ref, v_ref, wl0_ref, wl1_ref, al0_ref, al1_ref, kk_ref, ka_ref, rk_ref,
                    d0_ref, d1_ref, kz0_ref, kz1_ref, b0_ref, b1_ref, na_ref, bonus_ref, vdup_ref):
    nsl = r_ref.shape[1] // LANES
    sl = lambda ref, j: ref[:, j * LANES:(j + 1) * LANES]
    kks = [sl(k_ref, j) * sl(kk_ref, j) for j in range(nsl)]
    ss = _head_sum([x * x for x in kks])
    inv = lax.rsqrt(jnp.maximum(ss, 1e-24))
    bon = [None, None]
    for z, (wl_ref, al_ref, d_ref, kz_ref, b_ref) in enumerate(
            ((wl0_ref, al0_ref, d0_ref, kz0_ref, b0_ref), (wl1_ref, al1_ref, d1_ref, kz1_ref, b1_ref))):
        prods = []
        for j in range(nsl):
            kkn = kks[j] * inv
            if z == 0:
                na_ref[:, j * LANES:(j + 1) * LANES] = -kkn
            logw = -jax.nn.softplus(-sl(wl_ref, j)) - 0.5
            d_ref[:, j * LANES:(j + 1) * LANES] = jnp.exp(-jnp.exp(logw))
            a = jax.nn.sigmoid(sl(al_ref, j))
            kz = sl(k_ref, j) * (1.0 + (a - 1.0) * sl(ka_ref, j))
            kz_ref[:, j * LANES:(j + 1) * LANES] = kz
            b_ref[:, j * LANES:(j + 1) * LANES] = kkn * a
            prods.append(sl(r_ref, j) * kz * sl(rk_ref, j))
        bon[z] = _head_sum(prods)
    tot = bon[0] + bon[1]
    lane = lax.broadcasted_iota(jnp.int32, (1, LANES), 1)
    for j in range(nsl):
        vs = sl(v_ref, j)
        bonus_ref[:, j * LANES:(j + 1) * LANES] = tot * vs
        sw = pltpu.roll(vs, 64, 1)
        vdup_ref[:, (2 * j) * LANES:(2 * j + 1) * LANES] = jnp.where(lane < 64, vs, sw)
        vdup_ref[:, (2 * j + 1) * LANES:(2 * j + 2) * LANES] = jnp.where(lane < 64, sw, vs)


def _rwkv_prep(r, k, v, wl, al, k_k, k_a, r_k):
    t, d = r.shape
    tm = 64
    row = pl.BlockSpec((tm, d), lambda i: (i, 0))
    row2 = pl.BlockSpec((tm, 2 * d), lambda i: (i, 0))
    vec = pl.BlockSpec((1, d), lambda i: (0, 0))
    return pl.pallas_call(
        _rwkv_prep_body, grid=(t // tm,),
        in_specs=[row] * 7 + [vec] * 3,
        out_specs=[row] * 8 + [row2],
        out_shape=[jax.ShapeDtypeStruct((t, d), F32)] * 8 + [jax.ShapeDtypeStruct((t, 2 * d), F32)],
        name="rwkv_prep", compiler_params=_params(("parallel",)),
    )(r, k, v, wl[0], wl[1], al[0], al[1], k_k.reshape(1, d), k_a.reshape(1, d), r_k.reshape(1, d))


def _wkv_body(r_ref, w_ref, k_ref, a_ref, b_ref, v_ref, y_ref, s_ref, *, tb, reverse):
    @pl.when(pl.program_id(0) == 0)
    def _():
        s_ref[...] = jnp.zeros_like(s_ref)

    nv = s_ref.shape[0]

    def fold(x):
        rows = jnp.sum(x, axis=0, keepdims=True)
        return rows + pltpu.roll(rows, 64, 1)

    def step(i, carry):
        t = (tb - 1 - i) if reverse else i
        a2, w2, b2, k2, r2 = a_ref[t], w_ref[t], b_ref[t], k_ref[t], r_ref[t]
        for v in range(nv):
            st = s_ref[v]
            vrow = v_ref[t, pl.ds(v, 1), :]
            sa = fold(st * a2)
            sn = st * w2 + sa * b2 + vrow * k2
            s_ref[v] = sn
            y_ref[t, pl.ds(v, 1), :] = fold(sn * r2)
        return carry

    lax.fori_loop(0, tb, step, 0)


def _wkv_scan(r, w, k, a, b, vdup, s_lat, reverse):
    t, d = r.shape
    n = RWKV_HEAD
    nh = d // n
    tb = 64
    nb, nb_lat = t // tb, s_lat // tb
    nb_ctx = nb - nb_lat
    if reverse:
        order = lambda i: nb - 1 - i
    else:
        order = lambda i: jnp.where(i < nb_ctx, i + nb_lat, i - nb_ctx)
    kspec = pl.BlockSpec((tb, n // 2, 2 * nh), lambda i: (order(i), 0, 0))
    vspec = pl.BlockSpec((tb, n, 2 * nh), lambda i: (order(i), 0, 0))
    as3 = lambda z: z.reshape(t, n // 2, 2 * nh)
    body = functools.partial(_wkv_body, tb=tb, reverse=reverse)
    y = pl.pallas_call(
        body, grid=(nb,),
        in_specs=[kspec] * 5 + [vspec], out_specs=vspec,
        out_shape=jax.ShapeDtypeStruct((t, n, 2 * nh), F32),
        scratch_shapes=[pltpu.VMEM((n, n // 2, 2 * nh), F32)],
        name="wkv_rev" if reverse else "wkv_fwd",
        compiler_params=_params(("arbitrary",)),
    )(as3(r), as3(w), as3(k), as3(a), as3(b), vdup.reshape(t, n, 2 * nh))
    return y.reshape(t, 2 * d)


def _rwkv_finish_body(y0_ref, y1_ref, bonus_ref, g_ref, lw_ref, lb_ref, o_ref):
    nsl = bonus_ref.shape[1] // LANES
    lane = lax.broadcasted_iota(jnp.int32, (1, LANES), 1)
    ysl = lambda ref, j: ref[:, j * LANES:(j + 1) * LANES]
    ys = [jnp.where(lane < 64, ysl(y0_ref, 2 * j) + ysl(y1_ref, 2 * j), ysl(y0_ref, 2 * j + 1) + ysl(y1_ref, 2 * j + 1))
          for j in range(nsl)]
    mean = _head_sum(ys) * (1.0 / RWKV_HEAD)
    cs = [y - mean for y in ys]
    var = _head_sum([c * c for c in cs]) * (1.0 / RWKV_HEAD)
    inv = lax.rsqrt(var + RWKV_GN_EPS)
    for j in range(nsl):
        s = slice(j * LANES, (j + 1) * LANES)
        out = cs[j] * inv * lw_ref[:, s] + lb_ref[:, s] + bonus_ref[:, s]
        o_ref[:, s] = (out * g_ref[:, s].astype(F32)).astype(o_ref.dtype)


def _rwkv_finish(y0, y1, bonus, g, ln_w, ln_b):
    t, d = bonus.shape
    tm = 128
    row = pl.BlockSpec((tm, d), lambda i: (i, 0))
    row2 = pl.BlockSpec((tm, 2 * d), lambda i: (i, 0))
    vec = pl.BlockSpec((1, d), lambda i: (0, 0))
    return pl.pallas_call(
        _rwkv_finish_body, grid=(t // tm,), in_specs=[row2] * 2 + [row] * 2 + [vec] * 2, out_specs=row,
        out_shape=jax.ShapeDtypeStruct((t, d), BF16), name="rwkv_finish",
        compiler_params=_params(("parallel",)),
    )(y0, y1, bonus, g, ln_w.reshape(1, d), ln_b.reshape(1, d))


def _rwkv_mixer(h, p, m, s_lat):
    t, d = h.shape
    mixes = _rwkv_mix(h, p["rwkv_mu"][m], s_lat)
    w_rkv = _perm_cols(p["rwkv_w_rkv"][m]).astype(BF16)
    r = _matmul(mixes[0], w_rkv[0], out_dtype=F32, name="rwkv_r")
    k = _matmul(mixes[1], w_rkv[1], out_dtype=F32, name="rwkv_k")
    v = _matmul(mixes[2], w_rkv[2], out_dtype=F32, name="rwkv_v")
    lora = p["rwkv_w1"].shape[-1]
    w1 = jnp.concatenate([p["rwkv_w1"][m, 0], p["rwkv_w1"][m, 1]], axis=1).astype(BF16)
    a1 = jnp.concatenate([p["rwkv_a1"][m, 0], p["rwkv_a1"][m, 1]], axis=1).astype(BF16)
    wmid = _matmul(mixes[3], w1, out_dtype=BF16, epi=functools.partial(_epi_act, act=jnp.tanh), name="rwkv_w1")
    amid = _matmul(mixes[4], a1, out_dtype=BF16, name="rwkv_a1")
    w2 = _perm_cols(p["rwkv_w2"][m]).astype(BF16)
    a2 = _perm_cols(p["rwkv_a2"][m]).astype(BF16)
    w0 = _perm_cols(p["rwkv_w0"][m])
    a0 = _perm_cols(p["rwkv_a0"][m])
    wl, al = [], []
    for z in range(2):
        wl.append(_matmul(wmid[:, z * lora:(z + 1) * lora], w2[z], out_dtype=F32, epi=_epi_bias,
                          epi_ops=[_row_vec_op(w0[z], 1024)], tn=1024, name="rwkv_w2"))
        al.append(_matmul(amid[:, z * lora:(z + 1) * lora], a2[z], out_dtype=F32, epi=_epi_bias,
                          epi_ops=[_row_vec_op(a0[z], 1024)], tn=1024, name="rwkv_a2"))
    gmid = _matmul(mixes[5], p["rwkv_g1"][m].astype(BF16), out_dtype=BF16,
                   epi=functools.partial(_epi_act, act=jax.nn.sigmoid), name="rwkv_g1")
    g = _matmul(gmid, _perm_cols(p["rwkv_g2"][m]).astype(BF16), out_dtype=BF16, name="rwkv_g2")
    d0, d1, kz0, kz1, b0, b1, na, bonus, vdup = _rwkv_prep(
        r, k, v, wl, al, _perm_cols(p["rwkv_k_k"][m]), _perm_cols(p["rwkv_k_a"][m]), _perm_cols(p["rwkv_r_k"][m]))
    y0 = _wkv_scan(r, d0, kz0, na, b0, vdup, s_lat, False)
    y1 = _wkv_scan(r, d1, kz1, na, b1, vdup, s_lat, True)
    out = _rwkv_finish(y0, y1, bonus, g, _perm_cols(p["rwkv_ln_w"][m]), _perm_cols(p["rwkv_ln_b"][m]))
    w_o = p["rwkv_w_o"][m]
    nh = d // RWKV_HEAD
    w_o = w_o.reshape(nh, RWKV_HEAD, d).transpose(1, 0, 2).reshape(d, d).astype(BF16)
    return _matmul(out, w_o, out_dtype=BF16, name="rwkv_o")


def _epi_moe_act(acc, ids, b1_ref, gates_ref):
    e = ids[1]
    u = acc + b1_ref[0]
    x_glu = jnp.minimum(u[:, :EXPERT_FF], SWIGLU_LIMIT)
    x_lin = jnp.clip(u[:, EXPERT_FF:], -SWIGLU_LIMIT, SWIGLU_LIMIT)
    act = x_glu * jax.nn.sigmoid(SWIGLU_ALPHA * x_glu) * (x_lin + 1.0)
    gates = gates_ref[...]
    lane = lax.broadcasted_iota(jnp.int32, gates.shape, 1)
    gcol = jnp.sum(jnp.where(lane == e, gates, 0.0), axis=1, keepdims=True)
    return act * gcol


def _epi_moe_out(acc, ids, gates_ref, b2_ref):
    return acc + jnp.dot(gates_ref[...], b2_ref[...], preferred_element_type=F32)


def _moe(h2, gates, w1, b1, w2, b2):
    t, d = h2.shape
    e, _, f2 = w1.shape
    ff = f2 // 2
    tm = _row_tile(t, MM_ROWS)
    gate_spec = pl.BlockSpec((tm, e), lambda i, j, k: (i, 0))
    act = _matmul(
        h2, w1.astype(BF16), out_dtype=BF16, tm=tm, tn=f2, tk=d, n_total=e * f2,
        w_spec=pl.BlockSpec((None, d, f2), lambda i, j, k: (j, 0, 0)),
        epi=_epi_moe_act,
        epi_ops=[(b1.reshape(e, 1, f2), pl.BlockSpec((1, 1, f2), lambda i, j, k: (j, 0, 0))), (gates, gate_spec)],
        out_shape=jax.ShapeDtypeStruct((t, e * ff), BF16),
        out_spec=pl.BlockSpec((tm, ff), lambda i, j, k: (i, j)), name="moe_up")
    tn = 512
    return _matmul(
        act, w2.astype(BF16).reshape(e * ff, d), out_dtype=BF16, tm=tm, tn=tn, epi=_epi_moe_out,
        epi_ops=[(gates, gate_spec), (b2, pl.BlockSpec((e, tn), lambda i, j, k: (0, j)))], name="moe_down")


def kernel(x, c, ctx, c_ctx, ada_a, ada_b, ada_bias, norm_g, mla_w_dq, mla_q_norm, mla_w_uq, mla_w_dkv, mla_kv_norm, mla_w_ukv, mla_w_o, rwkv_mu, rwkv_w_rkv, rwkv_w0, rwkv_w1, rwkv_w2, rwkv_a0, rwkv_a1, rwkv_a2, rwkv_g1, rwkv_g2, rwkv_k_k, rwkv_k_a, rwkv_r_k, rwkv_ln_w, rwkv_ln_b, rwkv_w_o, swa_w_qkv, swa_b_qkv, swa_sink, swa_w_o, swa_b_o, moe_w_router, moe_b_router, moe_w1, moe_b1, moe_w2, moe_b2):
    p = dict(mla_w_dq=mla_w_dq, mla_q_norm=mla_q_norm, mla_w_uq=mla_w_uq, mla_w_dkv=mla_w_dkv,
             mla_kv_norm=mla_kv_norm, mla_w_ukv=mla_w_ukv, mla_w_o=mla_w_o,
             rwkv_mu=rwkv_mu, rwkv_w_rkv=rwkv_w_rkv, rwkv_w0=rwkv_w0, rwkv_w1=rwkv_w1, rwkv_w2=rwkv_w2,
             rwkv_a0=rwkv_a0, rwkv_a1=rwkv_a1, rwkv_a2=rwkv_a2, rwkv_g1=rwkv_g1, rwkv_g2=rwkv_g2,
             rwkv_k_k=rwkv_k_k, rwkv_k_a=rwkv_k_a, rwkv_r_k=rwkv_r_k, rwkv_ln_w=rwkv_ln_w,
             rwkv_ln_b=rwkv_ln_b, rwkv_w_o=rwkv_w_o, swa_w_qkv=swa_w_qkv, swa_b_qkv=swa_b_qkv,
             swa_sink=swa_sink, swa_w_o=swa_w_o, swa_b_o=swa_b_o)
    batch, s_lat, d = x.shape
    n_ctx = ctx.shape[1]
    assert batch == 1 and s_lat % n_ctx == 0 and n_ctx % EW_ROWS == 0
    depth = ada_a.shape[0]
    xs = jnp.concatenate([x[0], ctx[0]], axis=0)
    cond8 = jnp.concatenate([c, c_ctx[None, :], jnp.zeros((SUBLANES - 2, d), F32)], axis=0)
    rope_c, rope_s = _rope_tables(s_lat, n_ctx)
    mods = [_adaln(cond8, ada_a[i], ada_b[i], ada_bias[i]) for i in range(depth)]
    h = _norm_mod(xs, norm_g[0, 0], mods[0], s_lat)
    for i in range(depth):
        last = i == depth - 1
        kind, m = i % N_MIXERS, i // N_MIXERS
        if kind == 0:
            y = _mla_mixer(h, p, m, rope_c, rope_s, s_lat, not last)
        elif kind == 1:
            y = _rwkv_mixer(h, p, m, s_lat)
        else:
            y = _swa_mixer(h, p, m, rope_c, rope_s, s_lat, not last)
        xs, h2, gates = _resid_norm(xs, y, norm_g[i, 1], mods[i], 2, s_lat, gb=norm_g[i, 2], mods_b=mods[i],
                                    mod_rows=(3, 4), router=(moe_w_router[i], moe_b_router[i]))
        f = _moe(h2, gates, moe_w1[i], moe_b1[i], moe_w2[i], moe_b2[i])
        if last:
            (xs,) = _resid_norm(xs, f, norm_g[i, 3], mods[i], 5, s_lat)
        else:
            xs, h = _resid_norm(xs, f, norm_g[i, 3], mods[i], 5, s_lat, gb=norm_g[i + 1, 0],
                                mods_b=mods[i + 1], mod_rows=(0, 1))
    return xs[:s_lat][None]
```

```python
import functools

import jax
import jax.numpy as jnp
from jax import lax
from jax.experimental import pallas as pl
from jax.experimental.pallas import tpu as pltpu

F32 = jnp.float32
BF16 = jnp.bfloat16

GRID_W = 64
ROPE_BASE = 10000.0
NORM_EPS = 1e-6
MLA_HEADS = 64
MLA_KV_RANK = 512
MLA_NOPE_DIM = 128
MLA_ROPE_DIM = 64
MLA_V_DIM = 128
MLA_QK_DIM = MLA_NOPE_DIM + MLA_ROPE_DIM
RWKV_HEAD = 64
RWKV_GN_EPS = 64e-5
SWA_HEADS = 64
SWA_KV_HEADS = 8
SWA_HEAD_DIM = 64
SWA_WINDOW = 128
SWA_BLOCK = 128
N_EXPERTS = 32
TOP_K = 4
EXPERT_FF = 384
SWIGLU_ALPHA = 1.702
SWIGLU_LIMIT = 7.0
N_MIXERS = 3

LANES = 128
SUBLANES = 8
VMEM_LIMIT = 56 * 1024 * 1024
LOG2E = 1.4426950408889634

MM_ROWS = 1280
MM_K = 4096
EW_ROWS = 256
ATTN_Q_ROWS = 2048
ATTN_K_ROWS = 1024
ATTN_MAX_RISE = 64.0


def _params(sem):
    return pltpu.CompilerParams(dimension_semantics=sem, vmem_limit_bytes=VMEM_LIMIT)


def _row_tile(t, pref):
    if t <= pref:
        return t
    best = None
    for cand in range(pref, 7, -8):
        if t % cand == 0:
            best = cand
            break
    assert best is not None, (t, pref)
    return best


def _col_tile(n, pref):
    if n <= pref:
        return n
    for cand in range(pref, LANES - 1, -LANES):
        if n % cand == 0:
            return cand
    return n


def _mm_body(*refs, nk, n_epi, epi, store):
    x_ref, w_ref = refs[0], refs[1]
    epi_refs = refs[2:2 + n_epi]
    o_ref = refs[2 + n_epi]
    ids = (pl.program_id(0), pl.program_id(1))
    part = jnp.dot(x_ref[...].astype(BF16), w_ref[...], preferred_element_type=F32)
    if nk == 1:
        store(o_ref, epi(part, ids, *epi_refs))
        return
    acc_ref = refs[3 + n_epi]
    k = pl.program_id(2)

    @pl.when(k == 0)
    def _():
        acc_ref[...] = part

    @pl.when(k > 0)
    def _():
        acc_ref[...] += part

    @pl.when(k == nk - 1)
    def _():
        store(o_ref, epi(acc_ref[...], ids, *epi_refs))


def _store_plain(o_ref, val):
    o_ref[...] = val.astype(o_ref.dtype)


def _epi_none(acc, ids):
    return acc


def _matmul(x, w, *, out_dtype, tm=None, tn=None, tk=None, epi=_epi_none, epi_ops=(),
            w_spec=None, n_total=None, out_shape=None, out_spec=None, store=_store_plain, name="mm"):
    m, kdim = x.shape
    n = n_total if n_total is not None else w.shape[1]
    tm = tm or _row_tile(m, MM_ROWS)
    tk = tk or _col_tile(kdim, MM_K)
    tn = tn or _col_tile(n, 1024 if tk <= 1024 else 512)
    assert m % tm == 0 and n % tn == 0 and kdim % tk == 0, (m, n, kdim, tm, tn, tk)
    nk = kdim // tk
    grid = (m // tm, n // tn, nk)
    if w_spec is None:
        w_spec = pl.BlockSpec((tk, tn), lambda i, j, k: (k, j))
    in_specs = [pl.BlockSpec((tm, tk), lambda i, j, k: (i, k)), w_spec] + [s for _, s in epi_ops]
    if out_shape is None:
        out_shape = jax.ShapeDtypeStruct((m, n), out_dtype)
        out_spec = pl.BlockSpec((tm, tn), lambda i, j, k: (i, j))
    scratch = [pltpu.VMEM((tm, tn), F32)] if nk > 1 else []
    body = functools.partial(_mm_body, nk=nk, n_epi=len(epi_ops), epi=epi, store=store)
    return pl.pallas_call(
        body, grid=grid, in_specs=in_specs, out_specs=out_spec, out_shape=out_shape,
        scratch_shapes=scratch, name=name,
        compiler_params=_params(("parallel", "parallel", "arbitrary")),
    )(x, w, *[a for a, _ in epi_ops])


def _row_vec_op(vec, tn):
    return (vec.reshape(1, -1).astype(F32), pl.BlockSpec((1, tn), lambda i, j, k: (0, j)))


def _epi_bias(acc, ids, b_ref):
    return acc + b_ref[...]


def _epi_act(acc, ids, *, act):
    return act(acc)


def _epi_rmsnorm(acc, ids, g_ref):
    ms = jnp.mean(acc * acc, axis=-1, keepdims=True)
    return acc * lax.rsqrt(ms + NORM_EPS) * g_ref[...]


def _rope_cols(x, c, sg):
    lane = lax.broadcasted_iota(jnp.int32, (1, LANES), 1)
    first_half = (lane % 64) < 32
    outs = []
    for g in range(x.shape[1] // LANES):
        xg = x[:, g * LANES:(g + 1) * LANES]
        partner = jnp.where(first_half, pltpu.roll(xg, LANES - 32, 1), pltpu.roll(xg, 32, 1))
        outs.append(xg * c + partner * sg)
    return outs[0] if len(outs) == 1 else jnp.concatenate(outs, axis=1)


def _epi_rope(acc, ids, c_ref, s_ref):
    return _rope_cols(acc, c_ref[...], s_ref[...])


def _epi_bias_rope(acc, ids, b_ref, c_ref, s_ref):
    return _rope_cols(acc + b_ref[...], c_ref[...], s_ref[...])


def _rope_ops(rope_c, rope_s, tm):
    spec = pl.BlockSpec((tm, LANES), lambda i, j, k: (i, 0))
    return [(rope_c, spec), (rope_s, spec)]


def _rope_tables(s_lat, n_ctx):
    t = jnp.arange(s_lat)
    row = (t // GRID_W).astype(F32)
    col = (t % GRID_W).astype(F32)
    axis_dim = 32
    inv = ROPE_BASE ** (-jnp.arange(0, axis_dim, 2, dtype=F32) / axis_dim)
    ang = jnp.concatenate([row[:, None] * inv, col[:, None] * inv], axis=-1)
    cos, sin = jnp.cos(ang), jnp.sin(ang)
    c64 = jnp.concatenate([cos, cos], axis=-1)
    s64 = jnp.concatenate([-sin, sin], axis=-1)
    c = jnp.concatenate([c64, c64], axis=-1)
    s = jnp.concatenate([s64, s64], axis=-1)
    c = jnp.concatenate([c, jnp.ones((n_ctx, LANES), F32)], axis=0)
    s = jnp.concatenate([s, jnp.zeros((n_ctx, LANES), F32)], axis=0)
    return c, s


def _rms(x, g):
    return x * lax.rsqrt(jnp.mean(x * x, axis=-1, keepdims=True) + NORM_EPS) * g


def _norm_mod_body(x_ref, g_ref, mod_ref, h_ref):
    h = _rms(x_ref[...], g_ref[...]) * (1.0 + mod_ref[0, 1:2, :]) + mod_ref[0, 0:1, :]
    h_ref[...] = h.astype(h_ref.dtype)


def _norm_mod(x, g, mods, s_lat):
    t, d = x.shape
    tm = EW_ROWS
    nb_lat = s_lat // tm
    return pl.pallas_call(
        _norm_mod_body, grid=(t // tm,),
        in_specs=[pl.BlockSpec((tm, d), lambda i: (i, 0)),
                  pl.BlockSpec((1, d), lambda i: (0, 0)),
                  pl.BlockSpec((1, 6, d), lambda i: (jnp.where(i >= nb_lat, 1, 0), 0, 0))],
        out_specs=pl.BlockSpec((tm, d), lambda i: (i, 0)),
        out_shape=jax.ShapeDtypeStruct((t, d), BF16), name="norm_mod",
        compiler_params=_params(("parallel",)),
    )(x, g.reshape(1, d), mods)


def _top4_gates(logits):
    lane = lax.broadcasted_iota(jnp.int32, logits.shape, 1)
    work = logits
    gate = jnp.zeros_like(logits)
    denom = jnp.zeros((logits.shape[0], 1), F32)
    m0 = None
    for _ in range(TOP_K):
        m = jnp.max(work, axis=1, keepdims=True)
        idx = jnp.min(jnp.where(work == m, lane, N_EXPERTS), axis=1, keepdims=True)
        hit = lane == idx
        if m0 is None:
            m0 = m
        wk = jnp.exp(m - m0)
        gate = gate + jnp.where(hit, wk, 0.0)
        denom = denom + wk
        work = jnp.where(hit, -jnp.inf, work)
    return gate / denom


def _resid_body(*refs, gate_row, mod_rows, router):
    x_ref, y_ref, ga_ref, moda_ref = refs[:4]
    pos = 4
    xn = x_ref[...] + moda_ref[0, gate_row:gate_row + 1, :] * _rms(y_ref[...].astype(F32), ga_ref[...])
    if mod_rows is None:
        refs[pos][...] = xn
        return
    gb_ref, modb_ref = refs[pos], refs[pos + 1]
    pos += 2
    if router:
        wr_ref, br_ref = refs[pos], refs[pos + 1]
        pos += 2
    xn_ref, h_ref = refs[pos], refs[pos + 1]
    xn_ref[...] = xn
    sh, sc = mod_rows
    h = _rms(xn, gb_ref[...]) * (1.0 + modb_ref[0, sc:sc + 1, :]) + modb_ref[0, sh:sh + 1, :]
    h_ref[...] = h.astype(h_ref.dtype)
    if router:
        gates_ref = refs[pos + 2]
        logits = jnp.dot(h, wr_ref[...], precision=lax.Precision.HIGHEST,
                         preferred_element_type=F32) + br_ref[...]
        gates_ref[...] = _top4_gates(logits)


def _resid_norm(x, y, ga, mods_a, gate_row, s_lat, gb=None, mods_b=None, mod_rows=None, router=None):
    t, d = x.shape
    tm = EW_ROWS
    nb_lat = s_lat // tm
    row = pl.BlockSpec((tm, d), lambda i: (i, 0))
    vec = pl.BlockSpec((1, d), lambda i: (0, 0))
    mod = pl.BlockSpec((1, 6, d), lambda i: (jnp.where(i >= nb_lat, 1, 0), 0, 0))
    ins = [x, y, ga.reshape(1, d), mods_a]
    in_specs = [row, row, vec, mod]
    out_shape = [jax.ShapeDtypeStruct((t, d), F32)]
    out_specs = [row]
    if mod_rows is not None:
        ins += [gb.reshape(1, d), mods_b]
        in_specs += [vec, mod]
        out_shape.append(jax.ShapeDtypeStruct((t, d), BF16))
        out_specs.append(row)
        if router is not None:
            wr, br = router
            ins += [wr, br.reshape(1, N_EXPERTS)]
            in_specs += [pl.BlockSpec((d, N_EXPERTS), lambda i: (0, 0)),
                         pl.BlockSpec((1, N_EXPERTS), lambda i: (0, 0))]
            out_shape.append(jax.ShapeDtypeStruct((t, N_EXPERTS), F32))
            out_specs.append(pl.BlockSpec((tm, N_EXPERTS), lambda i: (i, 0)))
    body = functools.partial(_resid_body, gate_row=gate_row, mod_rows=mod_rows, router=router is not None)
    outs = pl.pallas_call(
        body, grid=(t // tm,), in_specs=in_specs, out_specs=out_specs, out_shape=out_shape,
        name="resid_norm", compiler_params=_params(("parallel",)),
    )(*ins)
    return outs


def _silu(x):
    return x * jax.nn.sigmoid(x)


def _adaln(cond8, a, b, bias):
    d = cond8.shape[1]
    sil = _silu(cond8)
    low = _matmul(sil, a.astype(BF16), out_dtype=F32, tm=8, name="ada_a")
    m = _matmul(low, b.astype(BF16), out_dtype=F32, tm=8, epi=_epi_bias,
                epi_ops=[_row_vec_op(bias, 1024)], tn=1024, name="ada_b")
    return m[:2].reshape(2, 6, d)


def _mla_attn_body(qn_ref, qr_ref, kn_ref, kpe_ref, v_ref, o_ref, k_scr, vt_scr, m_scr, l_scr, acc_scr,
                   *, scale, tk, n_main, tail):
    hd = pl.program_id(0)
    nkeys = k_scr.shape[0]
    vchunk = 256

    @pl.when(pl.program_id(1) == 0)
    def _():
        k_scr[:, :MLA_NOPE_DIM] = kn_ref[...]
        k_scr[:, MLA_NOPE_DIM:] = kpe_ref[:, :MLA_ROPE_DIM]

        def tr(c, carry):
            off = pl.multiple_of(c * vchunk, vchunk)
            vt_scr[:, pl.ds(off, vchunk)] = v_ref[pl.ds(off, vchunk), :].astype(F32).T.astype(BF16)
            return carry

        lax.fori_loop(0, nkeys // vchunk, tr, 0)

    c0 = scale * LOG2E
    qn_t = qn_ref[...].astype(F32).T * c0
    qr_t = qr_ref[...].astype(F32).T * c0
    qr_t = jnp.where(hd % 2 == 0, qr_t[:MLA_ROPE_DIM], qr_t[MLA_ROPE_DIM:])
    qt = jnp.concatenate([qn_t, qr_t], axis=0).astype(BF16)

    def two_pass(ks, vts):
        m = m_scr[...]
        s = jnp.dot(ks, qt, preferred_element_type=F32)
        m_new = jnp.maximum(m, jnp.max(s, axis=0, keepdims=True))
        alpha = jnp.exp2(m - m_new)
        p = jnp.exp2(s - m_new)
        l_scr[...] = alpha * l_scr[...] + jnp.sum(p, axis=0, keepdims=True)
        acc_scr[...] = alpha * acc_scr[...] + jnp.dot(vts, p.astype(BF16), preferred_element_type=F32)
        m_scr[...] = m_new

    def one_pass(c, carry):
        off = pl.multiple_of(c * tk, tk)
        m = m_scr[...]
        s = jnp.dot(k_scr[pl.ds(off, tk), :], qt, preferred_element_type=F32)
        p = jnp.exp2(s - m)
        cmax = jnp.max(s, axis=0, keepdims=True)
        psum = jnp.sum(p, axis=0, keepdims=True)
        pv = jnp.dot(vt_scr[:, pl.ds(off, tk)], p.astype(BF16), preferred_element_type=F32)
        safe = jnp.max(cmax - m) <= ATTN_MAX_RISE

        @pl.when(safe)
        def _():
            m_new = jnp.maximum(m, cmax)
            alpha = jnp.exp2(m - m_new)
            l_scr[...] = (l_scr[...] + psum) * alpha
            acc_scr[...] = (acc_scr[...] + pv) * alpha
            m_scr[...] = m_new

        @pl.when(jnp.logical_not(safe))
        def _():
            two_pass(k_scr[pl.ds(off, tk), :], vt_scr[:, pl.ds(off, tk)])

        return carry

    m_scr[...] = jnp.full_like(m_scr, -jnp.inf)
    l_scr[...] = jnp.zeros_like(l_scr)
    acc_scr[...] = jnp.zeros_like(acc_scr)
    if tail:
        lo = n_main * tk
        two_pass(k_scr[lo:lo + tail, :], vt_scr[:, lo:lo + tail])
        first = 0
    else:
        two_pass(k_scr[0:tk, :], vt_scr[:, 0:tk])
        first = 1
    if n_main > first:
        lax.fori_loop(first, n_main, one_pass, 0)
    o_ref[...] = (acc_scr[...] / l_scr[...]).T.astype(o_ref.dtype)


def _mla_attention(qn, qr, kv, kpe, *, q_rows, k_rows, tq, name):
    hh = MLA_HEADS
    q0, nq = q_rows
    k0, nkeys = k_rows
    assert q0 % tq == 0 and nq % tq == 0 and k0 % nkeys == 0 and nkeys % 256 == 0
    tk = min(ATTN_K_ROWS, nkeys)
    n_main = nkeys // tk
    tail = nkeys - n_main * tk
    body = functools.partial(_mla_attn_body, scale=MLA_QK_DIM ** -0.5, tk=tk, n_main=n_main, tail=tail)
    kb = k0 // nkeys
    once = pl.Buffered(1)
    return pl.pallas_call(
        body, grid=(hh, nq // tq),
        in_specs=[pl.BlockSpec((tq, MLA_NOPE_DIM), lambda h, i: (q0 // tq + i, h)),
                  pl.BlockSpec((tq, LANES), lambda h, i: (q0 // tq + i, h // 2)),
                  pl.BlockSpec((nkeys, MLA_NOPE_DIM), lambda h, i: (kb, 2 * h), pipeline_mode=once),
                  pl.BlockSpec((nkeys, LANES), lambda h, i: (kb, 0), pipeline_mode=once),
                  pl.BlockSpec((nkeys, MLA_V_DIM), lambda h, i: (kb, 2 * h + 1), pipeline_mode=once)],
        out_specs=pl.BlockSpec((tq, MLA_V_DIM), lambda h, i: (i, h)),
        out_shape=jax.ShapeDtypeStruct((nq, hh * MLA_V_DIM), BF16),
        scratch_shapes=[pltpu.VMEM((nkeys, MLA_QK_DIM), BF16), pltpu.VMEM((MLA_V_DIM, nkeys), BF16),
                        pltpu.VMEM((1, tq), F32), pltpu.VMEM((1, tq), F32), pltpu.VMEM((MLA_V_DIM, tq), F32)],
        name=name, compiler_params=_params(("parallel", "arbitrary")),
    )(qn, qr, kv, kpe, kv)


def _mla_mixer(h, p, m, rope_c, rope_s, s_lat, with_ctx):
    t, d = h.shape
    n_ctx = t - s_lat
    hh = MLA_HEADS
    tm = _row_tile(t, MM_ROWS)
    cq = _matmul(h, p["mla_w_dq"][m].astype(BF16), out_dtype=BF16, tn=p["mla_w_dq"].shape[2],
                 epi=_epi_rmsnorm, epi_ops=[_row_vec_op(p["mla_q_norm"][m], p["mla_w_dq"].shape[2])], name="mla_dq")
    w_dkv = p["mla_w_dkv"][m]
    ckv = _matmul(h, w_dkv[:, :MLA_KV_RANK].astype(BF16), out_dtype=BF16, tn=MLA_KV_RANK,
                  epi=_epi_rmsnorm, epi_ops=[_row_vec_op(p["mla_kv_norm"][m], MLA_KV_RANK)], name="mla_dkv")
    w_pe = w_dkv[:, MLA_KV_RANK:].astype(BF16)
    kpe = _matmul(h, jnp.concatenate([w_pe, w_pe], axis=1), out_dtype=BF16, tn=LANES,
                  epi=_epi_rope, epi_ops=_rope_ops(rope_c, rope_s, tm), name="mla_kpe")
    w_uq = p["mla_w_uq"][m].astype(BF16).reshape(-1, hh, MLA_QK_DIM)
    qn = _matmul(cq, w_uq[:, :, :MLA_NOPE_DIM].reshape(-1, hh * MLA_NOPE_DIM), out_dtype=BF16, name="mla_uq_nope")
    qr = _matmul(cq, w_uq[:, :, MLA_NOPE_DIM:].reshape(-1, hh * MLA_ROPE_DIM), out_dtype=BF16,
                 epi=_epi_rope, epi_ops=_rope_ops(rope_c, rope_s, tm), name="mla_uq_rope")
    kv = _matmul(ckv, p["mla_w_ukv"][m].astype(BF16), out_dtype=BF16, name="mla_ukv")
    tq = ATTN_Q_ROWS if s_lat % ATTN_Q_ROWS == 0 else n_ctx
    o = _mla_attention(qn, qr, kv, kpe, q_rows=(0, s_lat), k_rows=(0, t), tq=tq, name="mla_attn_lat")
    if with_ctx:
        o_c = _mla_attention(qn, qr, kv, kpe, q_rows=(s_lat, n_ctx), k_rows=(s_lat, n_ctx), tq=n_ctx,
                             name="mla_attn_ctx")
        o = jnp.concatenate([o, o_c], axis=0)
    else:
        o = jnp.concatenate([o, jnp.zeros((n_ctx, o.shape[1]), o.dtype)], axis=0)
    return _matmul(o, p["mla_w_o"][m].astype(BF16), out_dtype=BF16, name="mla_o")


def _swa_body(sink_ref, q_ref, kp_ref, kc_ref, kn_ref, kx_ref, vp_ref, vc_ref, vn_ref, vx_ref, o_ref,
              *, scale, nb, grp, latent):
    j = pl.program_id(0)
    kh = pl.program_id(1)
    blk = q_ref.shape[0]
    dh = SWA_HEAD_DIM
    q_t = q_ref[...].astype(F32).T * (scale * LOG2E)
    qt = jnp.concatenate([q_t[g * dh:(g + 1) * dh] for g in range(grp)], axis=1).astype(BF16)
    if latent:
        keys = jnp.concatenate([kx_ref[0], kp_ref[0], kc_ref[0], kn_ref[0]], axis=0)
        vals = jnp.concatenate([vx_ref[0], vp_ref[0], vc_ref[0], vn_ref[0]], axis=0)
        n_ctx = kx_ref.shape[1]
        nkeys = keys.shape[0]
        c = lax.broadcasted_iota(jnp.int32, (nkeys, blk), 0) - n_ctx
        r = lax.broadcasted_iota(jnp.int32, (nkeys, blk), 1)
        rel = c - blk - r
        kpos = (j - 1) * blk + c
        ok1 = (c < 0) | ((jnp.abs(rel) <= SWA_WINDOW) & (kpos >= 0) & (kpos < nb * blk))
        ok = jnp.concatenate([ok1] * grp, axis=1)
    else:
        keys, vals = kx_ref[0], vx_ref[0]
        ok = None
    s = jnp.dot(keys, qt, preferred_element_type=F32)
    if ok is not None:
        s = jnp.where(ok, s, -jnp.inf)
    sink = jnp.concatenate([jnp.full((1, blk), sink_ref[kh * grp + g], F32) for g in range(grp)], axis=1) * LOG2E
    m = jnp.maximum(jnp.max(s, axis=0, keepdims=True), sink)
    pr = jnp.exp2(s - m)
    denom = jnp.sum(pr, axis=0, keepdims=True) + jnp.exp2(sink - m)
    o_t = jnp.dot(vals.astype(F32).T.astype(BF16), pr.astype(BF16), preferred_element_type=F32) / denom
    o_all = jnp.concatenate([o_t[:, g * blk:(g + 1) * blk] for g in range(grp)], axis=0)
    o_ref[...] = o_all.T.astype(o_ref.dtype)


def _swa_attention(q, k, v, sink, s_lat, latent):
    t = q.shape[0]
    n_ctx = t - s_lat
    grp = SWA_HEADS // SWA_KV_HEADS
    blk = SWA_BLOCK
    dh = SWA_HEAD_DIM
    width = grp * dh
    if latent:
        nb = s_lat // blk
        q0 = 0
    else:
        nb = 1
        blk = n_ctx
        q0 = s_lat // n_ctx
    ctx_blk = s_lat // n_ctx

    def kv_spec(shift):
        return pl.BlockSpec((1, SWA_BLOCK, dh), lambda j, kh: (kh, jnp.clip(j + shift, 0, max(nb - 1, 0)), 0))

    ctx_spec = pl.BlockSpec((1, n_ctx, dh), lambda j, kh: (kh, ctx_blk, 0))
    body = functools.partial(_swa_body, scale=dh ** -0.5, nb=nb, grp=grp, latent=latent)
    win = [kv_spec(-1), kv_spec(0), kv_spec(1)]
    return pl.pallas_call(
        body, grid=(nb, SWA_KV_HEADS),
        in_specs=[pl.BlockSpec(memory_space=pltpu.SMEM),
                  pl.BlockSpec((blk, width), lambda j, kh: (q0 + j, kh))] + win + [ctx_spec] + win + [ctx_spec],
        out_specs=pl.BlockSpec((blk, width), lambda j, kh: (j, kh)),
        out_shape=jax.ShapeDtypeStruct((nb * blk, SWA_HEADS * dh), BF16),
        name="swa_attn_lat" if latent else "swa_attn_ctx",
        compiler_params=_params(("parallel", "parallel")),
    )(sink.astype(F32), q, k, k, k, k, v, v, v, v)


def _swa_mixer(h, p, m, rope_c, rope_s, s_lat, with_ctx):
    t, d = h.shape
    n_ctx = t - s_lat
    tm = _row_tile(t, MM_ROWS)
    qw = SWA_HEADS * SWA_HEAD_DIM
    kw = SWA_KV_HEADS * SWA_HEAD_DIM
    w = p["swa_w_qkv"][m].astype(BF16)
    b = p["swa_b_qkv"][m]
    tn = 512
    qk = _matmul(h, w[:, :qw + kw], out_dtype=BF16, tn=tn, epi=_epi_bias_rope,
                 epi_ops=[_row_vec_op(b[:qw + kw], tn)] + _rope_ops(rope_c, rope_s, tm), name="swa_qk")
    vv = _matmul(h, w[:, qw + kw:], out_dtype=BF16, tn=tn, epi=_epi_bias,
                 epi_ops=[_row_vec_op(b[qw + kw:], tn)], name="swa_v")
    q = qk[:, :qw]
    k = qk[:, qw:].reshape(t, SWA_KV_HEADS, SWA_HEAD_DIM).transpose(1, 0, 2)
    v = vv.reshape(t, SWA_KV_HEADS, SWA_HEAD_DIM).transpose(1, 0, 2)
    sink = p["swa_sink"][m]
    o = _swa_attention(q, k, v, sink, s_lat, True)
    if with_ctx:
        o = jnp.concatenate([o, _swa_attention(q, k, v, sink, s_lat, False)], axis=0)
    else:
        o = jnp.concatenate([o, jnp.zeros((n_ctx, o.shape[1]), o.dtype)], axis=0)
    return _matmul(o, p["swa_w_o"][m].astype(BF16), out_dtype=BF16, epi=_epi_bias,
                   epi_ops=[_row_vec_op(p["swa_b_o"][m], 512)], tn=512, name="swa_o")


def _perm_cols(w):
    lead = w.shape[:-1]
    nh = w.shape[-1] // RWKV_HEAD
    return jnp.swapaxes(w.reshape(lead + (nh, RWKV_HEAD)), -1, -2).reshape(lead + (nh * RWKV_HEAD,))


def _rwkv_mix_body(h_ref, hp_ref, hn_ref, mu_ref, *out_refs, nb_lat, nb):
    i = pl.program_id(0)
    h = h_ref[...].astype(F32)
    tm = h.shape[0]
    first = (i == 0) | (i == nb_lat)
    last = (i == nb_lat - 1) | (i == nb - 1)
    prev_row = jnp.where(first, 0.0, hp_ref[SUBLANES - 1:SUBLANES, :].astype(F32))
    next_row = jnp.where(last, 0.0, hn_ref[0:1, :].astype(F32))
    rid = lax.broadcasted_iota(jnp.int32, (tm, 1), 0)
    prev = jnp.where(rid == 0, prev_row, pltpu.roll(h, 1, 0))
    nxt = jnp.where(rid == tm - 1, next_row, pltpu.roll(h, tm - 1, 0))
    xx = 0.5 * (prev + nxt) - h
    for n, o_ref in enumerate(out_refs):
        o_ref[...] = (h + xx * mu_ref[n:n + 1, :]).astype(o_ref.dtype)


def _rwkv_mix(h, mu, s_lat):
    t, d = h.shape
    tm = 128
    nb, nb_lat = t // tm, s_lat // tm
    r8 = tm // SUBLANES
    body = functools.partial(_rwkv_mix_body, nb_lat=nb_lat, nb=nb)
    row = pl.BlockSpec((tm, d), lambda i: (i, 0))
    return pl.pallas_call(
        body, grid=(nb,),
        in_specs=[row,
                  pl.BlockSpec((SUBLANES, d), lambda i: (jnp.maximum(i * r8 - 1, 0), 0)),
                  pl.BlockSpec((SUBLANES, d), lambda i: (jnp.minimum((i + 1) * r8, t // SUBLANES - 1), 0)),
                  pl.BlockSpec((6, d), lambda i: (0, 0))],
        out_specs=[row] * 6, out_shape=[jax.ShapeDtypeStruct((t, d), BF16)] * 6, name="rwkv_mix",
        compiler_params=_params(("parallel",)),
    )(h, h, h, mu)


def _head_sum(ref_or_val_slabs):
    acc = ref_or_val_slabs[0]
    for s in ref_or_val_slabs[1:]:
        acc = acc + s
    return acc + pltpu.roll(acc, 64, 1)


def _rwkv_prep_body(r_ref, k_ref, v_ref, wl0_ref, wl1_ref, al0_ref, al1_ref, kk_ref, ka_ref, rk_ref,
                    d0_ref, d1_ref, kz0_ref, kz1_ref, b0_ref, b1_ref, na_ref, bonus_ref, vdup_ref):
    nsl = r_ref.shape[1] // LANES
    sl = lambda ref, j: ref[:, j * LANES:(j + 1) * LANES]
    kks = [sl(k_ref, j) * sl(kk_ref, j) for j in range(nsl)]
    ss = _head_sum([x * x for x in kks])
    inv = lax.rsqrt(jnp.maximum(ss, 1e-24))
    bon = [None, None]
    for z, (wl_ref, al_ref, d_ref, kz_ref, b_ref) in enumerate(
            ((wl0_ref, al0_ref, d0_ref, kz0_ref, b0_ref), (wl1_ref, al1_ref, d1_ref, kz1_ref, b1_ref))):
        prods = []
        for j in range(nsl):
            kkn = kks[j] * inv
            if z == 0:
                na_ref[:, j * LANES:(j + 1) * LANES] = -kkn
            logw = -jax.nn.softplus(-sl(wl_ref, j)) - 0.5
            d_ref[:, j * LANES:(j + 1) * LANES] = jnp.exp(-jnp.exp(logw))
            a = jax.nn.sigmoid(sl(al_ref, j))
            kz = sl(k_ref, j) * (1.0 + (a - 1.0) * sl(ka_ref, j))
            kz_ref[:, j * LANES:(j + 1) * LANES] = kz
            b_ref[:, j * LANES:(j + 1) * LANES] = kkn * a
            prods.append(sl(r_ref, j) * kz * sl(rk_ref, j))
        bon[z] = _head_sum(prods)
    tot = bon[0] + bon[1]
    lane = lax.broadcasted_iota(jnp.int32, (1, LANES), 1)
    for j in range(nsl):
        vs = sl(v_ref, j)
        bonus_ref[:, j * LANES:(j + 1) * LANES] = tot * vs
        sw = pltpu.roll(vs, 64, 1)
        vdup_ref[:, (2 * j) * LANES:(2 * j + 1) * LANES] = jnp.where(lane < 64, vs, sw)
        vdup_ref[:, (2 * j + 1) * LANES:(2 * j + 2) * LANES] = jnp.where(lane < 64, sw, vs)


def _rwkv_prep(r, k, v, wl, al, k_k, k_a, r_k):
    t, d = r.shape
    tm = 64
    row = pl.BlockSpec((tm, d), lambda i: (i, 0))
    row2 = pl.BlockSpec((tm, 2 * d), lambda i: (i, 0))
    vec = pl.BlockSpec((1, d), lambda i: (0, 0))
    return pl.pallas_call(
        _rwkv_prep_body, grid=(t // tm,),
        in_specs=[row] * 7 + [vec] * 3,
        out_specs=[row] * 8 + [row2],
        out_shape=[jax.ShapeDtypeStruct((t, d), F32)] * 8 + [jax.ShapeDtypeStruct((t, 2 * d), F32)],
        name="rwkv_prep", compiler_params=_params(("parallel",)),
    )(r, k, v, wl[0], wl[1], al[0], al[1], k_k.reshape(1, d), k_a.reshape(1, d), r_k.reshape(1, d))


def _wkv_body(r_ref, w_ref, k_ref, a_ref, b_ref, v_ref, y_ref, s_ref, *, tb, reverse):
    @pl.when(pl.program_id(0) == 0)
    def _():
        s_ref[...] = jnp.zeros_like(s_ref)

    nv = s_ref.shape[0]

    def fold(x):
        rows = jnp.sum(x, axis=0, keepdims=True)
        return rows + pltpu.roll(rows, 64, 1)

    def step(i, carry):
        t = (tb - 1 - i) if reverse else i
        a2, w2, b2, k2, r2 = a_ref[t], w_ref[t], b_ref[t], k_ref[t], r_ref[t]
        wr2 = w2 * r2
        br = fold(b2 * r2)
        kr = fold(k2 * r2)
        for v in range(nv):
            st = s_ref[v]
            vrow = v_ref[t, pl.ds(v, 1), :]
            sa = fold(st * a2)
            y0 = fold(st * wr2)
            s_ref[v] = st * w2 + sa * b2 + vrow * k2
            y_ref[t, pl.ds(v, 1), :] = y0 + sa * br + vrow * kr
        return carry

    lax.fori_loop(0, tb, step, 0)


def _wkv_scan(r, w, k, a, b, vdup, s_lat, reverse):
    t, d = r.shape
    n = RWKV_HEAD
    nh = d // n
    tb = 64
    nb, nb_lat = t // tb, s_lat // tb
    nb_ctx = nb - nb_lat
    if reverse:
        order = lambda i: nb - 1 - i
    else:
        order = lambda i: jnp.where(i < nb_ctx, i + nb_lat, i - nb_ctx)
    kspec = pl.BlockSpec((tb, n // 2, 2 * nh), lambda i: (order(i), 0, 0))
    vspec = pl.BlockSpec((tb, n, 2 * nh), lambda i: (order(i), 0, 0))
    as3 = lambda z: z.reshape(t, n // 2, 2 * nh)
    body = functools.partial(_wkv_body, tb=tb, reverse=reverse)
    y = pl.pallas_call(
        body, grid=(nb,),
        in_specs=[kspec] * 5 + [vspec], out_specs=vspec,
        out_shape=jax.ShapeDtypeStruct((t, n, 2 * nh), F32),
        scratch_shapes=[pltpu.VMEM((n, n // 2, 2 * nh), F32)],
        name="wkv_rev" if reverse else "wkv_fwd",
        compiler_params=_params(("arbitrary",)),
    )(as3(r), as3(w), as3(k), as3(a), as3(b), vdup.reshape(t, n, 2 * nh))
    return y.reshape(t, 2 * d)


def _rwkv_finish_body(y0_ref, y1_ref, bonus_ref, g_ref, lw_ref, lb_ref, o_ref):
    nsl = bonus_ref.shape[1] // LANES
    lane = lax.broadcasted_iota(jnp.int32, (1, LANES), 1)
    ysl = lambda ref, j: ref[:, j * LANES:(j + 1) * LANES]
    ys = [jnp.where(lane < 64, ysl(y0_ref, 2 * j) + ysl(y1_ref, 2 * j), ysl(y0_ref, 2 * j + 1) + ysl(y1_ref, 2 * j + 1))
          for j in range(nsl)]
    mean = _head_sum(ys) * (1.0 / RWKV_HEAD)
    cs = [y - mean for y in ys]
    var = _head_sum([c * c for c in cs]) * (1.0 / RWKV_HEAD)
    inv = lax.rsqrt(var + RWKV_GN_EPS)
    for j in range(nsl):
        s = slice(j * LANES, (j + 1) * LANES)
        out = cs[j] * inv * lw_ref[:, s] + lb_ref[:, s] + bonus_ref[:, s]
        o_ref[:, s] = (out * g_ref[:, s].astype(F32)).astype(o_ref.dtype)


def _rwkv_finish(y0, y1, bonus, g, ln_w, ln_b):
    t, d = bonus.shape
    tm = 128
    row = pl.BlockSpec((tm, d), lambda i: (i, 0))
    row2 = pl.BlockSpec((tm, 2 * d), lambda i: (i, 0))
    vec = pl.BlockSpec((1, d), lambda i: (0, 0))
    return pl.pallas_call(
        _rwkv_finish_body, grid=(t // tm,), in_specs=[row2] * 2 + [row] * 2 + [vec] * 2, out_specs=row,
        out_shape=jax.ShapeDtypeStruct((t, d), BF16), name="rwkv_finish",
        compiler_params=_params(("parallel",)),
    )(y0, y1, bonus, g, ln_w.reshape(1, d), ln_b.reshape(1, d))


def _rwkv_mixer(h, p, m, s_lat):
    t, d = h.shape
    mixes = _rwkv_mix(h, p["rwkv_mu"][m], s_lat)
    w_rkv = _perm_cols(p["rwkv_w_rkv"][m]).astype(BF16)
    r = _matmul(mixes[0], w_rkv[0], out_dtype=F32, name="rwkv_r")
    k = _matmul(mixes[1], w_rkv[1], out_dtype=F32, name="rwkv_k")
    v = _matmul(mixes[2], w_rkv[2], out_dtype=F32, name="rwkv_v")
    lora = p["rwkv_w1"].shape[-1]
    w1 = jnp.concatenate([p["rwkv_w1"][m, 0], p["rwkv_w1"][m, 1]], axis=1).astype(BF16)
    a1 = jnp.concatenate([p["rwkv_a1"][m, 0], p["rwkv_a1"][m, 1]], axis=1).astype(BF16)
    wmid = _matmul(mixes[3], w1, out_dtype=BF16, epi=functools.partial(_epi_act, act=jnp.tanh), name="rwkv_w1")
    amid = _matmul(mixes[4], a1, out_dtype=BF16, name="rwkv_a1")
    w2 = _perm_cols(p["rwkv_w2"][m]).astype(BF16)
    a2 = _perm_cols(p["rwkv_a2"][m]).astype(BF16)
    w0 = _perm_cols(p["rwkv_w0"][m])
    a0 = _perm_cols(p["rwkv_a0"][m])
    wl, al = [], []
    for z in range(2):
        wl.append(_matmul(wmid[:, z * lora:(z + 1) * lora], w2[z], out_dtype=F32, epi=_epi_bias,
                          epi_ops=[_row_vec_op(w0[z], 1024)], tn=1024, name="rwkv_w2"))
        al.append(_matmul(amid[:, z * lora:(z + 1) * lora], a2[z], out_dtype=F32, epi=_epi_bias,
                          epi_ops=[_row_vec_op(a0[z], 1024)], tn=1024, name="rwkv_a2"))
    gmid = _matmul(mixes[5], p["rwkv_g1"][m].astype(BF16), out_dtype=BF16,
                   epi=functools.partial(_epi_act, act=jax.nn.sigmoid), name="rwkv_g1")
    g = _matmul(gmid, _perm_cols(p["rwkv_g2"][m]).astype(BF16), out_dtype=BF16, name="rwkv_g2")
    d0, d1, kz0, kz1, b0, b1, na, bonus, vdup = _rwkv_prep(
        r, k, v, wl, al, _perm_cols(p["rwkv_k_k"][m]), _perm_cols(p["rwkv_k_a"][m]), _perm_cols(p["rwkv_r_k"][m]))
    y0 = _wkv_scan(r, d0, kz0, na, b0, vdup, s_lat, False)
    y1 = _wkv_scan(r, d1, kz1, na, b1, vdup, s_lat, True)
    out = _rwkv_finish(y0, y1, bonus, g, _perm_cols(p["rwkv_ln_w"][m]), _perm_cols(p["rwkv_ln_b"][m]))
    w_o = p["rwkv_w_o"][m]
    nh = d // RWKV_HEAD
    w_o = w_o.reshape(nh, RWKV_HEAD, d).transpose(1, 0, 2).reshape(d, d).astype(BF16)
    return _matmul(out, w_o, out_dtype=BF16, name="rwkv_o")


def _epi_moe_act(acc, ids, b1_ref, gates_ref):
    e = ids[1]
    u = acc + b1_ref[0]
    x_glu = jnp.minimum(u[:, :EXPERT_FF], SWIGLU_LIMIT)
    x_lin = jnp.clip(u[:, EXPERT_FF:], -SWIGLU_LIMIT, SWIGLU_LIMIT)
    act = x_glu * jax.nn.sigmoid(SWIGLU_ALPHA * x_glu) * (x_lin + 1.0)
    gates = gates_ref[...]
    lane = lax.broadcasted_iota(jnp.int32, gates.shape, 1)
    gcol = jnp.sum(jnp.where(lane == e, gates, 0.0), axis=1, keepdims=True)
    return act * gcol


def _epi_moe_out(acc, ids, gates_ref, b2_ref):
    return acc + jnp.dot(gates_ref[...], b2_ref[...], preferred_element_type=F32)


def _moe(h2, gates, w1, b1, w2, b2):
    t, d = h2.shape
    e, _, f2 = w1.shape
    ff = f2 // 2
    tm = _row_tile(t, MM_ROWS)
    gate_spec = pl.BlockSpec((tm, e), lambda i, j, k: (i, 0))
    act = _matmul(
        h2, w1.astype(BF16), out_dtype=BF16, tm=tm, tn=f2, tk=d, n_total=e * f2,
        w_spec=pl.BlockSpec((None, d, f2), lambda i, j, k: (j, 0, 0)),
        epi=_epi_moe_act,
        epi_ops=[(b1.reshape(e, 1, f2), pl.BlockSpec((1, 1, f2), lambda i, j, k: (j, 0, 0))), (gates, gate_spec)],
        out_shape=jax.ShapeDtypeStruct((t, e * ff), BF16),
        out_spec=pl.BlockSpec((tm, ff), lambda i, j, k: (i, j)), name="moe_up")
    tn = 512
    return _matmul(
        act, w2.astype(BF16).reshape(e * ff, d), out_dtype=BF16, tm=tm, tn=tn, epi=_epi_moe_out,
        epi_ops=[(gates, gate_spec), (b2, pl.BlockSpec((e, tn), lambda i, j, k: (0, j)))], name="moe_down")


def kernel(x, c, ctx, c_ctx, ada_a, ada_b, ada_bias, norm_g, mla_w_dq, mla_q_norm, mla_w_uq, mla_w_dkv, mla_kv_norm, mla_w_ukv, mla_w_o, rwkv_mu, rwkv_w_rkv, rwkv_w0, rwkv_w1, rwkv_w2, rwkv_a0, rwkv_a1, rwkv_a2, rwkv_g1, rwkv_g2, rwkv_k_k, rwkv_k_a, rwkv_r_k, rwkv_ln_w, rwkv_ln_b, rwkv_w_o, swa_w_qkv, swa_b_qkv, swa_sink, swa_w_o, swa_b_o, moe_w_router, moe_b_router, moe_w1, moe_b1, moe_w2, moe_b2):
    p = dict(mla_w_dq=mla_w_dq, mla_q_norm=mla_q_norm, mla_w_uq=mla_w_uq, mla_w_dkv=mla_w_dkv,
             mla_kv_norm=mla_kv_norm, mla_w_ukv=mla_w_ukv, mla_w_o=mla_w_o,
             rwkv_mu=rwkv_mu, rwkv_w_rkv=rwkv_w_rkv, rwkv_w0=rwkv_w0, rwkv_w1=rwkv_w1, rwkv_w2=rwkv_w2,
             rwkv_a0=rwkv_a0, rwkv_a1=rwkv_a1, rwkv_a2=rwkv_a2, rwkv_g1=rwkv_g1, rwkv_g2=rwkv_g2,
             rwkv_k_k=rwkv_k_k, rwkv_k_a=rwkv_k_a, rwkv_r_k=rwkv_r_k, rwkv_ln_w=rwkv_ln_w,
             rwkv_ln_b=rwkv_ln_b, rwkv_w_o=rwkv_w_o, swa_w_qkv=swa_w_qkv, swa_b_qkv=swa_b_qkv,
             swa_sink=swa_sink, swa_w_o=swa_w_o, swa_b_o=swa_b_o)
    batch, s_lat, d = x.shape
    n_ctx = ctx.shape[1]
    assert batch == 1 and s_lat % n_ctx == 0 and n_ctx % EW_ROWS == 0
    depth = ada_a.shape[0]
    xs = jnp.concatenate([x[0], ctx[0]], axis=0)
    cond8 = jnp.concatenate([c, c_ctx[None, :], jnp.zeros((SUBLANES - 2, d), F32)], axis=0)
    rope_c, rope_s = _rope_tables(s_lat, n_ctx)
    mods = [_adaln(cond8, ada_a[i], ada_b[i], ada_bias[i]) for i in range(depth)]
    h = _norm_mod(xs, norm_g[0, 0], mods[0], s_lat)
    for i in range(depth):
        last = i == depth - 1
        kind, m = i % N_MIXERS, i // N_MIXERS
        if kind == 0:
            y = _mla_mixer(h, p, m, rope_c, rope_s, s_lat, not last)
        elif kind == 1:
            y = _rwkv_mixer(h, p, m, s_lat)
        else:
            y = _swa_mixer(h, p, m, rope_c, rope_s, s_lat, not last)
        xs, h2, gates = _resid_norm(xs, y, norm_g[i, 1], mods[i], 2, s_lat, gb=norm_g[i, 2], mods_b=mods[i],
                                    mod_rows=(3, 4), router=(moe_w_router[i], moe_b_router[i]))
        f = _moe(h2, gates, moe_w1[i], moe_b1[i], moe_w2[i], moe_b2[i])
        if last:
            (xs,) = _resid_norm(xs, f, norm_g[i, 3], mods[i], 5, s_lat)
        else:
            xs, h = _resid_norm(xs, f, norm_g[i, 3], mods[i], 5, s_lat, gb=norm_g[i + 1, 0],
                                mods_b=mods[i + 1], mod_rows=(0, 1))
    return xs[:s_lat][None]
```
